```python
import jax, jax.numpy as jnp
from jax import lax
import numpy as np

D_MODEL = 1024
BATCH = 1
SEQ = 16384
DEPTH = 4

N_MIXERS = 2
N_MLA = (DEPTH + 1) // 2
N_GLA = DEPTH // 2

MLA_HEADS = 8
QK_NOPE = 128
QK_ROPE = 64
V_HEAD = 128
Q_LORA = 384
KV_LORA = 256
ROPE_THETA = 10000.0
Q_BLOCK = 128

GLA_HEADS = 4
GLA_DK = D_MODEL // 2
GLA_DV = D_MODEL
GLA_HEAD_K = GLA_DK // GLA_HEADS
GLA_HEAD_V = GLA_DV // GLA_HEADS
GATE_RANK = 16
GATE_NORMALIZER = 16.0
CHUNK = 64

D_FF = 4 * D_MODEL
EPS = 1e-6

kernel_name = "hybrid_mla_gla_sqrelu_trunk"


def rms_norm(x, gain):
    xf = x.astype(jnp.float32)
    y = xf * lax.rsqrt(jnp.mean(xf * xf, axis=-1, keepdims=True) + EPS)
    return (y * gain.astype(jnp.float32)).astype(x.dtype)


def rope_tables(positions):
    inv_freq = 1.0 / (ROPE_THETA ** (jnp.arange(0, QK_ROPE, 2, dtype=jnp.float32) / QK_ROPE))
    ang = positions.astype(jnp.float32)[..., None] * inv_freq
    return jnp.cos(ang), jnp.sin(ang)


def apply_rope(t, cos, sin):
    tf = t.astype(jnp.float32)
    half = QK_ROPE // 2
    t1, t2 = tf[..., :half], tf[..., half:]
    out = jnp.concatenate([t1 * cos - t2 * sin, t2 * cos + t1 * sin], axis=-1)
    return out.astype(t.dtype)


def mla_mixer(xn, positions, w_in, q_norm, w_uq, kv_norm, w_ukv, w_o):
    B, S, _ = xn.shape
    h = xn @ w_in
    c_q = h[..., :Q_LORA]
    c_kv = h[..., Q_LORA:Q_LORA + KV_LORA]
    k_rope = h[..., Q_LORA + KV_LORA:]
    q = (rms_norm(c_q, q_norm) @ w_uq).reshape(B, S, MLA_HEADS, QK_NOPE + QK_ROPE)
    kv = (rms_norm(c_kv, kv_norm) @ w_ukv).reshape(B, S, MLA_HEADS, QK_NOPE + V_HEAD)
    q_nope, q_rope = q[..., :QK_NOPE], q[..., QK_NOPE:]
    k_nope, v = kv[..., :QK_NOPE], kv[..., QK_NOPE:]
    cos, sin = rope_tables(positions)
    q_rope = apply_rope(q_rope, cos[:, :, None, :], sin[:, :, None, :])
    k_rope = apply_rope(k_rope, cos, sin)
    scale = (QK_NOPE + QK_ROPE) ** -0.5
    n_blocks = S // Q_BLOCK
    qn_b = q_nope.reshape(B, n_blocks, Q_BLOCK, MLA_HEADS, QK_NOPE).transpose(1, 0, 2, 3, 4)
    qr_b = q_rope.reshape(B, n_blocks, Q_BLOCK, MLA_HEADS, QK_ROPE).transpose(1, 0, 2, 3, 4)
    key_idx = jnp.arange(S)

    def attend_block(args):
        qn, qr, blk = args
        s = (jnp.einsum('bqhd,bkhd->bhqk', qn, k_nope).astype(jnp.float32)
             + jnp.einsum('bqhd,bkd->bhqk', qr, k_rope).astype(jnp.float32)) * scale
        q_idx = blk * Q_BLOCK + jnp.arange(Q_BLOCK)
        mask = key_idx[None, :] <= q_idx[:, None]
        s = jnp.where(mask, s, -jnp.inf)
        p = jax.nn.softmax(s, axis=-1).astype(v.dtype)
        return jnp.einsum('bhqk,bkhd->bqhd', p, v)

    o = lax.map(attend_block, (qn_b, qr_b, jnp.arange(n_blocks)))
    o = o.transpose(1, 0, 2, 3, 4).reshape(B, S, MLA_HEADS * V_HEAD)
    return o @ w_o


def gla_mixer(xn, w_in, w_gk_up, b_gk, g_norm, w_o):
    B, S, _ = xn.shape
    h = xn @ w_in
    q = h[..., :GLA_DK]
    k = h[..., GLA_DK:2 * GLA_DK]
    v = h[..., 2 * GLA_DK:2 * GLA_DK + GLA_DV]
    g = h[..., 2 * GLA_DK + GLA_DV:2 * GLA_DK + 2 * GLA_DV]
    a = h[..., 2 * GLA_DK + 2 * GLA_DV:]
    gk = jax.nn.log_sigmoid((a @ w_gk_up + b_gk).astype(jnp.float32)) / GATE_NORMALIZER
    nc = S // CHUNK

    def to_chunks(t, d):
        return t.astype(jnp.float32).reshape(B, nc, CHUNK, GLA_HEADS, d).transpose(1, 0, 3, 2, 4)

    qc = to_chunks(q, GLA_HEAD_K) * (GLA_HEAD_K ** -0.5)
    kc = to_chunks(k, GLA_HEAD_K)
    vc = to_chunks(v, GLA_HEAD_V)
    bc = jnp.cumsum(to_chunks(gk, GLA_HEAD_K), axis=3)
    causal = jnp.tril(jnp.ones((CHUNK, CHUNK), dtype=bool))

    def step(state, inp):
        q_, k_, v_, b_ = inp
        o_inter = jnp.einsum('bhcd,bhde->bhce', q_ * jnp.exp(b_), state)
        diff = b_[:, :, :, None, :] - b_[:, :, None, :, :]
        decay = jnp.exp(jnp.where(causal[:, :, None], diff, -jnp.inf))
        attn = jnp.einsum('bhtd,bhsd,bhtsd->bhts', q_, k_, decay)
        o_intra = jnp.einsum('bhts,bhse->bhte', attn, v_)
        b_last = b_[:, :, -1:, :]
        new_state = (state * jnp.exp(b_last[:, :, 0, :, None])
                     + jnp.einsum('bhsd,bhse->bhde', k_ * jnp.exp(b_last - b_), v_))
        return new_state, o_inter + o_intra

    state0 = jnp.zeros((B, GLA_HEADS, GLA_HEAD_K, GLA_HEAD_V), jnp.float32)
    _, o = lax.scan(step, state0, (qc, kc, vc, bc))
    o = o.transpose(1, 0, 3, 2, 4).reshape(B, S, GLA_HEADS, GLA_HEAD_V)
    o = rms_norm(o, g_norm)
    o = o * jax.nn.silu(g.astype(jnp.float32)).reshape(B, S, GLA_HEADS, GLA_HEAD_V)
    return o.reshape(B, S, GLA_DV).astype(xn.dtype) @ w_o


def sq_relu_mlp(xn, w_up, w_down):
    hid = jnp.square(jax.nn.relu(xn @ w_up))
    return hid @ w_down


def setup_inputs(seed: int = 0) -> dict:
    key = jax.random.key(seed)
    ks = jax.random.split(key, 20)

    def w(k, n, fan_in, fan_out):
        return jax.random.normal(k, (n, fan_in, fan_out), jnp.float32) * (fan_in ** -0.5)

    def gain(k, shape):
        return 1.0 + 0.01 * jax.random.normal(k, shape, jnp.float32)

    x = jax.random.normal(ks[0], (BATCH, SEQ, D_MODEL), jnp.float32)
    positions = jnp.broadcast_to(jnp.arange(SEQ, dtype=jnp.int32), (BATCH, SEQ))
    gla_in_width = 2 * GLA_DK + 2 * GLA_DV + GATE_RANK
    return {
        "x": x,
        "positions": positions,
        "norm_mix": gain(ks[1], (DEPTH, D_MODEL)),
        "norm_mlp": gain(ks[2], (DEPTH, D_MODEL)),
        "mla_w_in": w(ks[3], N_MLA, D_MODEL, Q_LORA + KV_LORA + QK_ROPE),
        "mla_q_norm": gain(ks[4], (N_MLA, Q_LORA)),
        "mla_w_uq": w(ks[5], N_MLA, Q_LORA, MLA_HEADS * (QK_NOPE + QK_ROPE)),
        "mla_kv_norm": gain(ks[6], (N_MLA, KV_LORA)),
        "mla_w_ukv": w(ks[7], N_MLA, KV_LORA, MLA_HEADS * (QK_NOPE + V_HEAD)),
        "mla_w_o": w(ks[8], N_MLA, MLA_HEADS * V_HEAD, D_MODEL),
        "gla_w_in": w(ks[9], N_GLA, D_MODEL, gla_in_width),
        "gla_w_gk_up": w(ks[10], N_GLA, GATE_RANK, GLA_DK),
        "gla_b_gk": 0.1 * jax.random.normal(ks[11], (N_GLA, GLA_DK), jnp.float32),
        "gla_g_norm": gain(ks[12], (N_GLA, GLA_HEAD_V)),
        "gla_w_o": w(ks[13], N_GLA, GLA_DV, D_MODEL),
        "mlp_w_up": w(ks[14], DEPTH, D_MODEL, D_FF),
        "mlp_w_down": w(ks[15], DEPTH, D_FF, D_MODEL),
        "final_norm": gain(ks[16], (D_MODEL,)),
    }


def reference(x, positions, norm_mix, norm_mlp, mla_w_in, mla_q_norm, mla_w_uq, mla_kv_norm,
              mla_w_ukv, mla_w_o, gla_w_in, gla_w_gk_up, gla_b_gk, gla_g_norm, gla_w_o,
              mlp_w_up, mlp_w_down, final_norm):
    for i in range(DEPTH):
        xn = rms_norm(x, norm_mix[i])
        j = i // N_MIXERS
        if i % N_MIXERS == 0:
            mix = mla_mixer(xn, positions, mla_w_in[j], mla_q_norm[j], mla_w_uq[j],
                            mla_kv_norm[j], mla_w_ukv[j], mla_w_o[j])
        else:
            mix = gla_mixer(xn, gla_w_in[j], gla_w_gk_up[j], gla_b_gk[j], gla_g_norm[j], gla_w_o[j])
        x = x + mix.astype(x.dtype)
        x = x + sq_relu_mlp(rms_norm(x, norm_mlp[i]), mlp_w_up[i], mlp_w_down[i]).astype(x.dtype)
    return rms_norm(x, final_norm)
```

```python
import functools
import math

import jax
import jax.numpy as jnp
import numpy as np
from jax import lax
from jax.experimental import pallas as pl
from jax.experimental.pallas import tpu as pltpu

F32 = jnp.float32
BF16 = jnp.bfloat16

D_MODEL = 1024
DEPTH = 4
MLA_HEADS = 8
QK_NOPE = 128
QK_ROPE = 64
V_HEAD = 128
Q_LORA = 384
KV_LORA = 256
ROPE_THETA = 10000.0
GLA_HEADS = 4
GLA_DK = 512
GLA_DV = 1024
GLA_HEAD_K = 128
GLA_HEAD_V = 256
GATE_RANK = 16
GATE_NORMALIZER = 16.0
D_FF = 4096
EPS = 1e-6

LANES = 128
QK_PAD = 256
VMEM_LIMIT = 56 * 1024 * 1024

TM_PROJ = 256
TQ = 512
TM_POST = 512
GLA_BLOCK = 256
FF_CHUNK = 1024

NT_DIMS = (((1,), (1,)), ((), ()))


def _dot(a, b):
    return jnp.dot(a, b, preferred_element_type=F32)


def _dot_nt(a, b):
    return lax.dot_general(a, b, NT_DIMS, preferred_element_type=F32)


def _rms(x, gain):
    ms = jnp.mean(x * x, axis=-1, keepdims=True)
    return x * lax.rsqrt(ms + EPS) * gain


def _params(*sem):
    return pltpu.CompilerParams(dimension_semantics=sem, vmem_limit_bytes=VMEM_LIMIT)


def _const_spec(shape):
    nd = len(shape)
    return pl.BlockSpec(shape, lambda *_: (0,) * nd, pipeline_mode=pl.Buffered(1))


def _rope_kernel(pos_ref, invf_ref, sgn_ref, cos_ref, sin_ref):
    ang = pos_ref[...].astype(F32) * invf_ref[...]
    cos_ref[...] = jnp.cos(ang)
    sin_ref[...] = jnp.sin(ang) * sgn_ref[...]


def _rope_tables(positions):
    s = positions.shape[0]
    lane = np.arange(LANES)
    inv_freq = 1.0 / (ROPE_THETA ** (jnp.arange(0, QK_ROPE, 2, dtype=F32) / QK_ROPE))
    invf = inv_freq[lane % (QK_ROPE // 2)].reshape(1, LANES)
    sgn = jnp.asarray(np.where(lane % QK_ROPE < QK_ROPE // 2, -1.0, 1.0), F32).reshape(1, LANES)
    tm = 1024
    return pl.pallas_call(
        _rope_kernel,
        grid=(s // tm,),
        in_specs=[pl.BlockSpec((tm, 1), lambda i: (i, 0)),
                  pl.BlockSpec((1, LANES), lambda i: (0, 0)),
                  pl.BlockSpec((1, LANES), lambda i: (0, 0))],
        out_specs=[pl.BlockSpec((tm, LANES), lambda i: (i, 0))] * 2,
        out_shape=[jax.ShapeDtypeStruct((s, LANES), F32)] * 2,
        compiler_params=_params("parallel"),
        name="rope_tables",
    )(positions.reshape(s, 1), invf, sgn)


def _mla_proj_kernel(x_ref, g_ref, win_ref, qn_ref, kvn_ref, wq_ref, wk_ref, wvt_ref, cos_ref, sin_ref,
                     q_out, k_out, vt_out, *, q_scale):
    xn = _rms(x_ref[...], g_ref[...]).astype(BF16)
    h = _dot(xn, win_ref[...])
    cq = _rms(h[:, :Q_LORA], qn_ref[...]).astype(BF16)
    ckv = _rms(h[:, Q_LORA:Q_LORA + KV_LORA], kvn_ref[...]).astype(BF16)
    cosv = cos_ref[...]
    sinv = sin_ref[...]
    r0 = Q_LORA + KV_LORA
    k_rot = (h[:, r0:r0 + LANES] * cosv + h[:, r0 + LANES:r0 + 2 * LANES] * sinv).astype(BF16)
    qb = _dot(cq, wq_ref[...])
    kn = _dot(ckv, wk_ref[...])
    hw = MLA_HEADS * LANES
    for hd in range(MLA_HEADS):
        c = slice(hd * LANES, (hd + 1) * LANES)
        q_out[hd, :, 0:LANES] = (qb[:, c] * q_scale).astype(BF16)
        q_rot = qb[:, hw + hd * LANES:hw + (hd + 1) * LANES] * cosv \
            + qb[:, 2 * hw + hd * LANES:2 * hw + (hd + 1) * LANES] * sinv
        q_out[hd, :, LANES:QK_PAD] = (q_rot * q_scale).astype(BF16)
        k_out[hd, :, 0:LANES] = kn[:, c].astype(BF16)
        k_out[hd, :, LANES:QK_PAD] = k_rot
    vt_out[0] = _dot_nt(wvt_ref[...], ckv).astype(BF16)


def _pad_rope_cols(w, n_heads):
    k = w.shape[0]
    w = w.reshape(k, n_heads, QK_ROPE)
    half = QK_ROPE // 2
    sw = jnp.concatenate([w[..., half:], w[..., :half]], axis=-1)
    z = jnp.zeros((k, n_heads, LANES - QK_ROPE), w.dtype)
    plain = jnp.concatenate([w, z], axis=-1).reshape(k, n_heads * LANES)
    swapped = jnp.concatenate([sw, z], axis=-1).reshape(k, n_heads * LANES)
    return plain, swapped


def _mla_proj(x, gain, w_in, q_norm, w_uq, kv_norm, w_ukv, cos_t, sin_t):
    s = x.shape[0]
    tm = TM_PROJ
    r0 = Q_LORA + KV_LORA
    kr, kr_sw = _pad_rope_cols(w_in[:, r0:], 1)
    win = jnp.concatenate([w_in[:, :r0], kr, kr_sw], axis=1).astype(BF16)
    wq3 = w_uq.reshape(Q_LORA, MLA_HEADS, QK_NOPE + QK_ROPE)
    qr, qr_sw = _pad_rope_cols(wq3[..., QK_NOPE:].reshape(Q_LORA, MLA_HEADS * QK_ROPE), MLA_HEADS)
    wq = jnp.concatenate([wq3[..., :QK_NOPE].reshape(Q_LORA, -1), qr, qr_sw], axis=1).astype(BF16)
    wkv3 = w_ukv.reshape(KV_LORA, MLA_HEADS, QK_NOPE + V_HEAD)
    wk = wkv3[..., :QK_NOPE].reshape(KV_LORA, -1).astype(BF16)
    wvt = wkv3[..., QK_NOPE:].reshape(KV_LORA, -1).T.astype(BF16)
    q_scale = (QK_NOPE + QK_ROPE) ** -0.5 * math.log2(math.e)
    nt = s // tm
    return pl.pallas_call(
        functools.partial(_mla_proj_kernel, q_scale=q_scale),
        grid=(nt,),
        in_specs=[pl.BlockSpec((tm, D_MODEL), lambda i: (i, 0)),
                  _const_spec((1, D_MODEL)),
                  _const_spec(win.shape),
                  _const_spec((1, Q_LORA)),
                  _const_spec((1, KV_LORA)),
                  _const_spec(wq.shape),
                  _const_spec(wk.shape),
                  _const_spec(wvt.shape),
                  pl.BlockSpec((tm, LANES), lambda i: (i, 0)),
                  pl.BlockSpec((tm, LANES), lambda i: (i, 0))],
        out_specs=[pl.BlockSpec((MLA_HEADS, tm, QK_PAD), lambda i: (0, i, 0)),
                   pl.BlockSpec((MLA_HEADS, tm, QK_PAD), lambda i: (0, i, 0)),
                   pl.BlockSpec((1, MLA_HEADS * V_HEAD, tm), lambda i: (i, 0, 0))],
        out_shape=[jax.ShapeDtypeStruct((MLA_HEADS, s, QK_PAD), BF16),
                   jax.ShapeDtypeStruct((MLA_HEADS, s, QK_PAD), BF16),
                   jax.ShapeDtypeStruct((nt, MLA_HEADS * V_HEAD, tm), BF16)],
        compiler_params=_params("parallel"),
        name="mla_proj",
    )(x, gain.reshape(1, -1), win, q_norm.reshape(1, -1), kv_norm.reshape(1, -1), wq, wk, wvt, cos_t, sin_t)


def _flash_kernel(q_ref, k_ref, vt_ref, o_ref, m_sc, l_sc, acc_sc):
    qi = pl.program_id(1)
    tk = TM_PROJ
    q = q_ref[0]
    m_sc[...] = jnp.full(m_sc.shape, -jnp.inf, F32)
    l_sc[...] = jnp.zeros(l_sc.shape, F32)
    acc_sc[...] = jnp.zeros(acc_sc.shape, F32)

    def step(j, masked):
        start = pl.multiple_of(j * tk, tk)
        kb = k_ref[0, pl.ds(start, tk), :]
        st = _dot_nt(kb, q)
        if masked:
            kidx = j * tk + lax.broadcasted_iota(jnp.int32, st.shape, 0)
            qidx = qi * TQ + lax.broadcasted_iota(jnp.int32, st.shape, 1)
            st = jnp.where(kidx <= qidx, st, -jnp.inf)
        m_old = m_sc[...]
        m_new = jnp.maximum(m_old, jnp.max(st, axis=0, keepdims=True))
        alpha = jnp.exp2(m_old - m_new)
        p = jnp.exp2(st - m_new)
        l_sc[...] = alpha * l_sc[...] + jnp.sum(p, axis=0, keepdims=True)
        acc_sc[...] = alpha * acc_sc[...] + _dot(vt_ref[j], p.astype(BF16))
        m_sc[...] = m_new

    n_sub = TQ // tk

    def full_body(j, carry):
        step(j, False)
        return carry

    lax.fori_loop(0, qi * n_sub, full_body, 0)
    for d in range(n_sub):
        step(qi * n_sub + d, True)
    o = acc_sc[...] * (1.0 / l_sc[...])
    o_ref[...] = o.T.astype(BF16)


def _flash(q, k, vt):
    s = q.shape[1]
    tk = TM_PROJ
    nkb = s // tk
    return pl.pallas_call(
        _flash_kernel,
        grid=(MLA_HEADS, s // TQ),
        in_specs=[pl.BlockSpec((1, TQ, QK_PAD), lambda h, i: (h, i, 0)),
                  pl.BlockSpec((1, s, QK_PAD), lambda h, i: (h, 0, 0)),
                  pl.BlockSpec((nkb, V_HEAD, tk), lambda h, i: (0, h, 0))],
        out_specs=pl.BlockSpec((TQ, V_HEAD), lambda h, i: (i, h)),
        out_shape=jax.ShapeDtypeStruct((s, MLA_HEADS * V_HEAD), BF16),
        scratch_shapes=[pltpu.VMEM((1, TQ), F32), pltpu.VMEM((1, TQ), F32), pltpu.VMEM((V_HEAD, TQ), F32)],
        compiler_params=_params("parallel", "arbitrary"),
        name="mla_flash",
    )(q, k, vt)


def _post_kernel(*refs, gla, final):
    if gla:
        x_ref, o_ref, gt_ref, gn_ref = refs[:4]
        refs = refs[4:]
    else:
        x_ref, o_ref = refs[:2]
        refs = refs[2:]
    wo_ref, nm_ref, wup_ref, wdn_ref = refs[:4]
    fn_ref = refs[4] if final else None
    out_ref = refs[-1]

    if gla:
        o = o_ref[...]
        gt = gt_ref[...]
        gn = gn_ref[...]
        parts = []
        for hd in range(GLA_HEADS):
            c = slice(hd * GLA_HEAD_V, (hd + 1) * GLA_HEAD_V)
            g = gt[:, c]
            parts.append((_rms(o[:, c], gn) * (g * (1.0 / (1.0 + jnp.exp(-g))))).astype(BF16))
        ob = jnp.concatenate(parts, axis=1)
    else:
        ob = o_ref[...]
    x1 = x_ref[...] + _dot(ob, wo_ref[...])
    xn = _rms(x1, nm_ref[...]).astype(BF16)
    y = jnp.zeros_like(x1)
    for c in range(D_FF // FF_CHUNK):
        hid = _dot(xn, wup_ref[:, c * FF_CHUNK:(c + 1) * FF_CHUNK])
        hid = jnp.square(jnp.maximum(hid, 0.0)).astype(BF16)
        y = y + _dot(hid, wdn_ref[c * FF_CHUNK:(c + 1) * FF_CHUNK, :])
    x2 = x1 + y
    if final:
        x2 = _rms(x2, fn_ref[...])
    out_ref[...] = x2


def _post(x, o, w_o, norm_mlp, w_up, w_down, *, gate=None, g_norm=None, final_norm=None):
    s = x.shape[0]
    tm = TM_POST
    gla = gate is not None
    final = final_norm is not None
    row = lambda w: pl.BlockSpec((tm, w), lambda i: (i, 0))
    args = [x, o]
    specs = [row(D_MODEL), row(D_MODEL)]
    if gla:
        args += [gate, g_norm.reshape(1, -1)]
        specs += [row(GLA_DV), _const_spec((1, GLA_HEAD_V))]
    args += [w_o.astype(BF16), norm_mlp.reshape(1, -1), w_up.astype(BF16), w_down.astype(BF16)]
    specs += [_const_spec(w_o.shape), _const_spec((1, D_MODEL)), _const_spec(w_up.shape), _const_spec(w_down.shape)]
    if final:
        args.append(final_norm.reshape(1, -1))
        specs.append(_const_spec((1, D_MODEL)))
    return pl.pallas_call(
        functools.partial(_post_kernel, gla=gla, final=final),
        grid=(s // tm,),
        in_specs=specs,
        out_specs=row(D_MODEL),
        out_shape=jax.ShapeDtypeStruct((s, D_MODEL), F32),
        compiler_params=_params("parallel"),
        name="post_gla" if gla else "post_mla",
    )(*args)


GLA_LEVELS = tuple(GLA_BLOCK >> (i + 1) for i in range(int(math.log2(GLA_BLOCK))))


def _gla_tables():
    n = GLA_BLOCK
    t = np.arange(n)
    u = np.arange(n)[None, :]
    mats = [(u <= t[:, None])]
    for w in GLA_LEVELS:
        m = (t // (2 * w)) * (2 * w) + w
        upper = (t % (2 * w)) >= w
        g = np.where(upper[:, None], (u > m[:, None]) & (u <= t[:, None]),
                     (u > t[:, None]) & (u <= m[:, None]))
        mats.append(g)
    gmat = np.stack(mats).astype(np.float32)
    x = t[:, None] ^ t[None, :]
    lvl = np.where(t[:, None] > t[None, :], np.floor(np.log2(np.maximum(x, 1))), np.where(x == 0, -1, -2))
    return jnp.asarray(gmat, BF16), jnp.asarray(lvl, jnp.int32)


def _gla_proj_kernel(x_ref, g_ref, wmain_ref, wa_ref, wgk_ref, bgk_ref, gmat_ref, lvl_ref, wvt_ref,
                     qi_out, kt_out, vt_out, gate_out, dec_out, oin_out):
    n = GLA_BLOCK
    dk = GLA_DK
    xn = _rms(x_ref[...], g_ref[...]).astype(BF16)
    hm = _dot(xn, wmain_ref[...])
    a = _dot(xn, wa_ref[...])
    z = _dot(a.astype(BF16), wgk_ref[...]) + bgk_ref[...]
    gk = (jnp.minimum(z, 0.0) - jnp.log(1.0 + jnp.exp(-jnp.abs(z)))) * (1.0 / GATE_NORMALIZER)
    hi = gk.astype(BF16)
    lo = (gk - hi.astype(F32)).astype(BF16)
    hilo = jnp.concatenate([hi, lo], axis=1)

    def decay_sum(idx):
        e = _dot(gmat_ref[idx], hilo)
        return e[:, :dk] + e[:, dk:]

    b = decay_sum(0)
    b_last = b[n - 1:n, :]
    dec_out[0] = jnp.exp(b_last)
    q = hm[:, :dk] * (GLA_HEAD_K ** -0.5)
    k = hm[:, dk:2 * dk]
    vb = hm[:, 2 * dk:2 * dk + GLA_DV].astype(BF16)
    qi_out[...] = (q * jnp.exp(b)).astype(BF16)
    kt_out[...] = (k * jnp.exp(b_last - b)).astype(BF16)
    gate_out[...] = hm[:, 2 * dk + GLA_DV:]
    vt_out[0] = _dot_nt(wvt_ref[...], xn).astype(BF16)

    lvl = lvl_ref[...]
    row = lax.broadcasted_iota(jnp.int32, (n, GLA_HEAD_K), 0)
    qb = q.astype(BF16)
    kb = k.astype(BF16)
    att = []
    for hd in range(GLA_HEADS):
        c = slice(hd * GLA_HEAD_K, (hd + 1) * GLA_HEAD_K)
        att.append(jnp.where(lvl == -1, _dot_nt(qb[:, c], kb[:, c]), 0.0))
    for li, w in enumerate(GLA_LEVELS):
        e = jnp.exp(decay_sum(1 + li))
        upper = (row & (2 * w - 1)) >= w
        for hd in range(GLA_HEADS):
            c = slice(hd * GLA_HEAD_K, (hd + 1) * GLA_HEAD_K)
            xs = (jnp.where(upper, q[:, c], k[:, c]) * e[:, c]).astype(BF16)
            att[hd] = jnp.where(lvl == int(math.log2(w)), _dot_nt(xs, xs), att[hd])
    for hd in range(GLA_HEADS):
        cv = slice(hd * GLA_HEAD_V, (hd + 1) * GLA_HEAD_V)
        oin_out[:, cv] = _dot(att[hd].astype(BF16), vb[:, cv])


def _gla_proj(x, gain, w_in, w_gk_up, b_gk):
    s = x.shape[0]
    n = GLA_BLOCK
    nb = s // n
    main_w = 2 * GLA_DK + 2 * GLA_DV
    wmain = w_in[:, :main_w].astype(BF16)
    wa = jnp.pad(w_in[:, main_w:], ((0, 0), (0, LANES - GATE_RANK))).astype(BF16)
    wgk = jnp.pad(w_gk_up, ((0, LANES - GATE_RANK), (0, 0))).astype(BF16)
    wvt = w_in[:, 2 * GLA_DK:2 * GLA_DK + GLA_DV].T.astype(BF16)
    gmat, lvl = _gla_tables()
    row = lambda w: pl.BlockSpec((n, w), lambda i: (i, 0))
    return pl.pallas_call(
        _gla_proj_kernel,
        grid=(nb,),
        in_specs=[row(D_MODEL), _const_spec((1, D_MODEL)), _const_spec(wmain.shape), _const_spec(wa.shape),
                  _const_spec(wgk.shape), _const_spec((1, GLA_DK)), _const_spec(gmat.shape), _const_spec(lvl.shape),
                  _const_spec(wvt.shape)],
        out_specs=[row(GLA_DK), row(GLA_DK), pl.BlockSpec((1, GLA_DV, n), lambda i: (i, 0, 0)), row(GLA_DV),
                   pl.BlockSpec((1, 1, GLA_DK), lambda i: (i, 0, 0)), row(GLA_DV)],
        out_shape=[jax.ShapeDtypeStruct((s, GLA_DK), BF16), jax.ShapeDtypeStruct((s, GLA_DK), BF16),
                   jax.ShapeDtypeStruct((nb, GLA_DV, n), BF16), jax.ShapeDtypeStruct((s, GLA_DV), F32),
                   jax.ShapeDtypeStruct((nb, 1, GLA_DK), F32), jax.ShapeDtypeStruct((s, GLA_DV), F32)],
        compiler_params=_params("parallel"),
        name="gla_proj",
    )(x, gain.reshape(1, -1), wmain, wa, wgk, b_gk.reshape(1, -1), gmat, lvl, wvt)


def _gla_scan_kernel(qi_ref, kt_ref, vt_ref, dec_ref, oin_ref, o_ref, st_sc):
    @pl.when(pl.program_id(0) == 0)
    def _():
        st_sc[...] = jnp.zeros(st_sc.shape, F32)

    qi = qi_ref[...]
    kt = kt_ref[...]
    dec = dec_ref[0]
    for hd in range(GLA_HEADS):
        ck = slice(hd * GLA_HEAD_K, (hd + 1) * GLA_HEAD_K)
        cv = slice(hd * GLA_HEAD_V, (hd + 1) * GLA_HEAD_V)
        st = st_sc[hd]
        o_ref[:, cv] = oin_ref[:, cv] + _dot_nt(qi[:, ck], st.astype(BF16))
        st_sc[hd] = st * dec[:, ck] + _dot(vt_ref[0, cv, :], kt[:, ck])


def _gla_scan(qi, kt, vt, dec, oin):
    s = qi.shape[0]
    n = GLA_BLOCK
    row = lambda w: pl.BlockSpec((n, w), lambda i: (i, 0))
    return pl.pallas_call(
        _gla_scan_kernel,
        grid=(s // n,),
        in_specs=[row(GLA_DK), row(GLA_DK), pl.BlockSpec((1, GLA_DV, n), lambda i: (i, 0, 0)),
                  pl.BlockSpec((1, 1, GLA_DK), lambda i: (i, 0, 0)), row(GLA_DV)],
        out_specs=row(GLA_DV),
        out_shape=jax.ShapeDtypeStruct((s, GLA_DV), F32),
        scratch_shapes=[pltpu.VMEM((GLA_HEADS, GLA_HEAD_V, GLA_HEAD_K), F32)],
        compiler_params=_params("arbitrary"),
        name="gla_scan",
    )(qi, kt, vt, dec, oin)


def kernel(x, positions, norm_mix, norm_mlp, mla_w_in, mla_q_norm, mla_w_uq, mla_kv_norm, mla_w_ukv, mla_w_o,
           gla_w_in, gla_w_gk_up, gla_b_gk, gla_g_norm, gla_w_o, mlp_w_up, mlp_w_down, final_norm):
    b, s, d = x.shape
    outs = []
    for bi in range(b):
        xb = x[bi]
        cos_t, sin_t = _rope_tables(positions[bi])
        for i in range(DEPTH):
            j = i // 2
            fin = final_norm if i == DEPTH - 1 else None
            if i % 2 == 0:
                q, k, vt = _mla_proj(xb, norm_mix[i], mla_w_in[j], mla_q_norm[j], mla_w_uq[j], mla_kv_norm[j],
                                     mla_w_ukv[j], cos_t, sin_t)
                o = _flash(q, k, vt)
                xb = _post(xb, o, mla_w_o[j], norm_mlp[i], mlp_w_up[i], mlp_w_down[i], final_norm=fin)
            else:
                qi, kt, vt, gate, dec, oin = _gla_proj(xb, norm_mix[i], gla_w_in[j], gla_w_gk_up[j], gla_b_gk[j])
                o = _gla_scan(qi, kt, vt, dec, oin)
                xb = _post(xb, o, gla_w_o[j], norm_mlp[i], mlp_w_up[i], mlp_w_down[i], gate=gate,
                           g_norm=gla_g_norm[j], final_norm=fin)
        outs.append(xb)
    return jnp.stack(outs)
```

```python
import functools
import math

import jax
import jax.numpy as jnp
import numpy as np
from jax import lax
from jax.experimental import pallas as pl
from jax.experimental.pallas import tpu as pltpu

F32 = jnp.float32
BF16 = jnp.bfloat16

D_MODEL = 1024
DEPTH = 4
MLA_HEADS = 8
QK_NOPE = 128
QK_ROPE = 64
V_HEAD = 128
Q_LORA = 384
KV_LORA = 256
ROPE_THETA = 10000.0
GLA_HEADS = 4
GLA_DK = 512
GLA_DV = 1024
GLA_HEAD_K = 128
GLA_HEAD_V = 256
GATE_RANK = 16
GATE_NORMALIZER = 16.0
D_FF = 4096
EPS = 1e-6

LANES = 128
QK_PAD = 256
VMEM_LIMIT = 56 * 1024 * 1024

TM_PROJ = 256
TQ = 1024
QG = 256
TM_POST = 512
GLA_BLOCK = 256
FF_CHUNK = 1024

NT_DIMS = (((1,), (1,)), ((), ()))


def _dot(a, b):
    return jnp.dot(a, b, preferred_element_type=F32)


def _dot_nt(a, b):
    return lax.dot_general(a, b, NT_DIMS, preferred_element_type=F32)


def _rms(x, gain):
    ms = jnp.mean(x * x, axis=-1, keepdims=True)
    return x * lax.rsqrt(ms + EPS) * gain


def _params(*sem):
    return pltpu.CompilerParams(dimension_semantics=sem, vmem_limit_bytes=VMEM_LIMIT)


def _const_spec(shape):
    nd = len(shape)
    return pl.BlockSpec(shape, lambda *_: (0,) * nd, pipeline_mode=pl.Buffered(1))


def _rope_kernel(pos_ref, invf_ref, sgn_ref, cos_ref, sin_ref):
    ang = pos_ref[...].astype(F32) * invf_ref[...]
    cos_ref[...] = jnp.cos(ang)
    sin_ref[...] = jnp.sin(ang) * sgn_ref[...]


def _rope_tables(positions):
    s = positions.shape[0]
    lane = np.arange(LANES)
    inv_freq = 1.0 / (ROPE_THETA ** (jnp.arange(0, QK_ROPE, 2, dtype=F32) / QK_ROPE))
    invf = inv_freq[lane % (QK_ROPE // 2)].reshape(1, LANES)
    sgn = jnp.asarray(np.where(lane % QK_ROPE < QK_ROPE // 2, -1.0, 1.0), F32).reshape(1, LANES)
    tm = 1024
    return pl.pallas_call(
        _rope_kernel,
        grid=(s // tm,),
        in_specs=[pl.BlockSpec((tm, 1), lambda i: (i, 0)),
                  pl.BlockSpec((1, LANES), lambda i: (0, 0)),
                  pl.BlockSpec((1, LANES), lambda i: (0, 0))],
        out_specs=[pl.BlockSpec((tm, LANES), lambda i: (i, 0))] * 2,
        out_shape=[jax.ShapeDtypeStruct((s, LANES), F32)] * 2,
        compiler_params=_params("parallel"),
        name="rope_tables",
    )(positions.reshape(s, 1), invf, sgn)


def _mla_proj_kernel(x_ref, g_ref, win_ref, qn_ref, kvn_ref, wq_ref, wk_ref, wvt_ref, cos_ref, sin_ref,
                     q_out, k_out, vt_out, *, q_scale):
    xn = _rms(x_ref[...], g_ref[...]).astype(BF16)
    h = _dot(xn, win_ref[...])
    cq = _rms(h[:, :Q_LORA], qn_ref[...]).astype(BF16)
    ckv = _rms(h[:, Q_LORA:Q_LORA + KV_LORA], kvn_ref[...]).astype(BF16)
    cosv = cos_ref[...]
    sinv = sin_ref[...]
    r0 = Q_LORA + KV_LORA
    k_rot = (h[:, r0:r0 + LANES] * cosv + h[:, r0 + LANES:r0 + 2 * LANES] * sinv).astype(BF16)
    qb = _dot(cq, wq_ref[...])
    kn = _dot(ckv, wk_ref[...])
    hw = MLA_HEADS * LANES
    for hd in range(MLA_HEADS):
        c = slice(hd * LANES, (hd + 1) * LANES)
        q_out[hd, :, 0:LANES] = (qb[:, c] * q_scale).astype(BF16)
        q_rot = qb[:, hw + hd * LANES:hw + (hd + 1) * LANES] * cosv \
            + qb[:, 2 * hw + hd * LANES:2 * hw + (hd + 1) * LANES] * sinv
        q_out[hd, :, LANES:QK_PAD] = (q_rot * q_scale).astype(BF16)
        k_out[hd, :, 0:LANES] = kn[:, c].astype(BF16)
        k_out[hd, :, LANES:QK_PAD] = k_rot
    vt_out[0] = _dot_nt(wvt_ref[...], ckv).astype(BF16)


def _pad_rope_cols(w, n_heads):
    k = w.shape[0]
    w = w.reshape(k, n_heads, QK_ROPE)
    half = QK_ROPE // 2
    sw = jnp.concatenate([w[..., half:], w[..., :half]], axis=-1)
    z = jnp.zeros((k, n_heads, LANES - QK_ROPE), w.dtype)
    plain = jnp.concatenate([w, z], axis=-1).reshape(k, n_heads * LANES)
    swapped = jnp.concatenate([sw, z], axis=-1).reshape(k, n_heads * LANES)
    return plain, swapped


def _mla_proj(x, gain, w_in, q_norm, w_uq, kv_norm, w_ukv, cos_t, sin_t):
    s = x.shape[0]
    tm = TM_PROJ
    r0 = Q_LORA + KV_LORA
    kr, kr_sw = _pad_rope_cols(w_in[:, r0:], 1)
    win = jnp.concatenate([w_in[:, :r0], kr, kr_sw], axis=1).astype(BF16)
    wq3 = w_uq.reshape(Q_LORA, MLA_HEADS, QK_NOPE + QK_ROPE)
    qr, qr_sw = _pad_rope_cols(wq3[..., QK_NOPE:].reshape(Q_LORA, MLA_HEADS * QK_ROPE), MLA_HEADS)
    wq = jnp.concatenate([wq3[..., :QK_NOPE].reshape(Q_LORA, -1), qr, qr_sw], axis=1).astype(BF16)
    wkv3 = w_ukv.reshape(KV_LORA, MLA_HEADS, QK_NOPE + V_HEAD)
    wk = wkv3[..., :QK_NOPE].reshape(KV_LORA, -1).astype(BF16)
    wvt = wkv3[..., QK_NOPE:].reshape(KV_LORA, -1).T.astype(BF16)
    q_scale = (QK_NOPE + QK_ROPE) ** -0.5 * math.log2(math.e)
    nt = s // tm
    return pl.pallas_call(
        functools.partial(_mla_proj_kernel, q_scale=q_scale),
        grid=(nt,),
        in_specs=[pl.BlockSpec((tm, D_MODEL), lambda i: (i, 0)),
                  _const_spec((1, D_MODEL)),
                  _const_spec(win.shape),
                  _const_spec((1, Q_LORA)),
                  _const_spec((1, KV_LORA)),
                  _const_spec(wq.shape),
                  _const_spec(wk.shape),
                  _const_spec(wvt.shape),
                  pl.BlockSpec((tm, LANES), lambda i: (i, 0)),
                  pl.BlockSpec((tm, LANES), lambda i: (i, 0))],
        out_specs=[pl.BlockSpec((MLA_HEADS, tm, QK_PAD), lambda i: (0, i, 0)),
                   pl.BlockSpec((MLA_HEADS, tm, QK_PAD), lambda i: (0, i, 0)),
                   pl.BlockSpec((1, MLA_HEADS * V_HEAD, tm), lambda i: (i, 0, 0))],
        out_shape=[jax.ShapeDtypeStruct((MLA_HEADS, s, QK_PAD), BF16),
                   jax.ShapeDtypeStruct((MLA_HEADS, s, QK_PAD), BF16),
                   jax.ShapeDtypeStruct((nt, MLA_HEADS * V_HEAD, tm), BF16)],
        compiler_params=_params("parallel"),
        name="mla_proj",
    )(x, gain.reshape(1, -1), win, q_norm.reshape(1, -1), kv_norm.reshape(1, -1), wq, wk, wvt, cos_t, sin_t)


def _flash_kernel(q_ref, k_ref, vt_ref, o_ref, m_sc, l_sc, acc_sc, sa_sc, sb_sc, cma_sc, cmb_sc):
    qi = pl.program_id(1)
    tk = TM_PROJ
    n_sub = TQ // tk
    groups = [slice(g * QG, (g + 1) * QG) for g in range(TQ // QG)]
    m_sc[...] = jnp.full(m_sc.shape, -jnp.inf, F32)
    l_sc[...] = jnp.zeros(l_sc.shape, F32)
    acc_sc[...] = jnp.zeros(acc_sc.shape, F32)

    def scores(kstep, buf, cols, diag):
        s_ref, cm_ref = buf
        start = pl.multiple_of(kstep * TQ, TQ)
        st = _dot_nt(k_ref[0, pl.ds(start, TQ), :], q_ref[0, cols, :])
        if diag:
            kidx = lax.broadcasted_iota(jnp.int32, st.shape, 0)
            qidx = cols.start + lax.broadcasted_iota(jnp.int32, st.shape, 1)
            st = jnp.where(kidx <= qidx, st, -jnp.inf)
        s_ref[:, cols] = st
        cm_ref[:, cols] = jnp.max(st, axis=0, keepdims=True)

    def consume(kstep, buf, cols):
        s_ref, cm_ref = buf
        m_old = m_sc[:, cols]
        m_new = jnp.maximum(m_old, cm_ref[:, cols])
        alpha = jnp.exp2(m_old - m_new)
        p = jnp.exp2(s_ref[:, cols] - m_new)
        l_sc[:, cols] = alpha * l_sc[:, cols] + jnp.sum(p, axis=0, keepdims=True)
        pb = p.astype(BF16)
        pv = _dot(vt_ref[kstep * n_sub], pb[0:tk])
        for jj in range(1, n_sub):
            pv = pv + _dot(vt_ref[kstep * n_sub + jj], pb[jj * tk:(jj + 1) * tk])
        acc_sc[:, cols] = alpha * acc_sc[:, cols] + pv
        m_sc[:, cols] = m_new

    def pipelined(knew, new_buf, kold, old_buf):
        for cols in groups:
            scores(knew, new_buf, cols, False)
            consume(kold, old_buf, cols)

    buf_a = (sa_sc, cma_sc)
    buf_b = (sb_sc, cmb_sc)
    for cols in groups:
        scores(qi, buf_a, cols, True)

    def body(t, carry):
        pipelined(2 * t, buf_b, jnp.where(t == 0, qi, 2 * t - 1), buf_a)
        pipelined(2 * t + 1, buf_a, 2 * t, buf_b)
        return carry

    lax.fori_loop(0, qi // 2, body, 0)

    @pl.when(qi % 2 == 1)
    def _():
        pipelined(qi - 1, buf_b, jnp.where(qi == 1, qi, qi - 2), buf_a)
        for cols in groups:
            consume(qi - 1, buf_b, cols)

    @pl.when(qi % 2 == 0)
    def _():
        for cols in groups:
            consume(jnp.where(qi == 0, qi, qi - 1), buf_a, cols)

    o = acc_sc[...] * (1.0 / l_sc[...])
    o_ref[...] = o.T.astype(BF16)


def _flash(q, k, vt):
    s = q.shape[1]
    tk = TM_PROJ
    nkb = s // tk
    return pl.pallas_call(
        _flash_kernel,
        grid=(MLA_HEADS, s // TQ),
        in_specs=[pl.BlockSpec((1, TQ, QK_PAD), lambda h, i: (h, i, 0)),
                  pl.BlockSpec((1, s, QK_PAD), lambda h, i: (h, 0, 0)),
                  pl.BlockSpec((nkb, V_HEAD, tk), lambda h, i: (0, h, 0))],
        out_specs=pl.BlockSpec((TQ, V_HEAD), lambda h, i: (i, h)),
        out_shape=jax.ShapeDtypeStruct((s, MLA_HEADS * V_HEAD), BF16),
        scratch_shapes=[pltpu.VMEM((1, TQ), F32), pltpu.VMEM((1, TQ), F32), pltpu.VMEM((V_HEAD, TQ), F32),
                        pltpu.VMEM((TQ, TQ), F32), pltpu.VMEM((TQ, TQ), F32),
                        pltpu.VMEM((1, TQ), F32), pltpu.VMEM((1, TQ), F32)],
        compiler_params=_params("parallel", "arbitrary"),
        name="mla_flash",
    )(q, k, vt)


def _post_kernel(*refs, gla, final):
    if gla:
        x_ref, o_ref, gt_ref, gn_ref = refs[:4]
        refs = refs[4:]
    else:
        x_ref, o_ref = refs[:2]
        refs = refs[2:]
    wo_ref, nm_ref, wup_ref, wdn_ref = refs[:4]
    fn_ref = refs[4] if final else None
    out_ref = refs[-1]

    if gla:
        o = o_ref[...]
        gt = gt_ref[...]
        gn = gn_ref[...]
        parts = []
        for hd in range(GLA_HEADS):
            c = slice(hd * GLA_HEAD_V, (hd + 1) * GLA_HEAD_V)
            g = gt[:, c]
            parts.append((_rms(o[:, c], gn) * (g * (1.0 / (1.0 + jnp.exp(-g))))).astype(BF16))
        ob = jnp.concatenate(parts, axis=1)
    else:
        ob = o_ref[...]
    x1 = x_ref[...] + _dot(ob, wo_ref[...])
    xn = _rms(x1, nm_ref[...]).astype(BF16)
    y = jnp.zeros_like(x1)
    for c in range(D_FF // FF_CHUNK):
        hid = _dot(xn, wup_ref[:, c * FF_CHUNK:(c + 1) * FF_CHUNK])
        hid = jnp.square(jnp.maximum(hid, 0.0)).astype(BF16)
        y = y + _dot(hid, wdn_ref[c * FF_CHUNK:(c + 1) * FF_CHUNK, :])
    x2 = x1 + y
    if final:
        x2 = _rms(x2, fn_ref[...])
    out_ref[...] = x2


def _post(x, o, w_o, norm_mlp, w_up, w_down, *, gate=None, g_norm=None, final_norm=None):
    s = x.shape[0]
    tm = TM_POST
    gla = gate is not None
    final = final_norm is not None
    row = lambda w: pl.BlockSpec((tm, w), lambda i: (i, 0))
    args = [x, o]
    specs = [row(D_MODEL), row(D_MODEL)]
    if gla:
        args += [gate, g_norm.reshape(1, -1)]
        specs += [row(GLA_DV), _const_spec((1, GLA_HEAD_V))]
    args += [w_o.astype(BF16), norm_mlp.reshape(1, -1), w_up.astype(BF16), w_down.astype(BF16)]
    specs += [_const_spec(w_o.shape), _const_spec((1, D_MODEL)), _const_spec(w_up.shape), _const_spec(w_down.shape)]
    if final:
        args.append(final_norm.reshape(1, -1))
        specs.append(_const_spec((1, D_MODEL)))
    return pl.pallas_call(
        functools.partial(_post_kernel, gla=gla, final=final),
        grid=(s // tm,),
        in_specs=specs,
        out_specs=row(D_MODEL),
        out_shape=jax.ShapeDtypeStruct((s, D_MODEL), F32),
        compiler_params=_params("parallel"),
        name="post_gla" if gla else "post_mla",
    )(*args)


GLA_LEVELS = tuple(GLA_BLOCK >> (i + 1) for i in range(int(math.log2(GLA_BLOCK))))


def _gla_tables():
    n = GLA_BLOCK
    t = np.arange(n)
    u = np.arange(n)[None, :]
    mats = [(u <= t[:, None])]
    for w in GLA_LEVELS:
        m = (t // (2 * w)) * (2 * w) + w
        upper = (t % (2 * w)) >= w
        g = np.where(upper[:, None], (u > m[:, None]) & (u <= t[:, None]),
                     (u > t[:, None]) & (u <= m[:, None]))
        mats.append(g)
    gmat = np.stack(mats).astype(np.float32)
    x = t[:, None] ^ t[None, :]
    lvl = np.where(t[:, None] > t[None, :], np.floor(np.log2(np.maximum(x, 1))), np.where(x == 0, -1, -2))
    return jnp.asarray(gmat, BF16), jnp.asarray(lvl, jnp.int32)


def _gla_proj_kernel(x_ref, g_ref, wmain_ref, wa_ref, wgk_ref, bgk_ref, gmat_ref, lvl_ref, wvt_ref,
                     qi_out, kt_out, vt_out, gate_out, dec_out, oin_out):
    n = GLA_BLOCK
    dk = GLA_DK
    xn = _rms(x_ref[...], g_ref[...]).astype(BF16)
    hm = _dot(xn, wmain_ref[...])
    a = _dot(xn, wa_ref[...])
    z = _dot(a.astype(BF16), wgk_ref[...]) + bgk_ref[...]
    gk = (jnp.minimum(z, 0.0) - jnp.log(1.0 + jnp.exp(-jnp.abs(z)))) * (1.0 / GATE_NORMALIZER)
    hi = gk.astype(BF16)
    lo = (gk - hi.astype(F32)).astype(BF16)
    hilo = jnp.concatenate([hi, lo], axis=1)

    def decay_sum(idx):
        e = _dot(gmat_ref[idx], hilo)
        return e[:, :dk] + e[:, dk:]

    b = decay_sum(0)
    b_last = b[n - 1:n, :]
    dec_out[0] = jnp.exp(b_last)
    q = hm[:, :dk] * (GLA_HEAD_K ** -0.5)
    k = hm[:, dk:2 * dk]
    vb = hm[:, 2 * dk:2 * dk + GLA_DV].astype(BF16)
    qi_out[...] = (q * jnp.exp(b)).astype(BF16)
    kt_out[...] = (k * jnp.exp(b_last - b)).astype(BF16)
    gate_out[...] = hm[:, 2 * dk + GLA_DV:]
    vt_out[0] = _dot_nt(wvt_ref[...], xn).astype(BF16)

    lvl = lvl_ref[...]
    row = lax.broadcasted_iota(jnp.int32, (n, GLA_HEAD_K), 0)
    qb = q.astype(BF16)
    kb = k.astype(BF16)
    att = []
    for hd in range(GLA_HEADS):
        c = slice(hd * GLA_HEAD_K, (hd + 1) * GLA_HEAD_K)
        att.append(jnp.where(lvl == -1, _dot_nt(qb[:, c], kb[:, c]), 0.0))
    for li, w in enumerate(GLA_LEVELS):
        e = jnp.exp(decay_sum(1 + li))
        upper = (row & (2 * w - 1)) >= w
        for hd in range(GLA_HEADS):
            c = slice(hd * GLA_HEAD_K, (hd + 1) * GLA_HEAD_K)
            xs = (jnp.where(upper, q[:, c], k[:, c]) * e[:, c]).astype(BF16)
            att[hd] = jnp.where(lvl == int(math.log2(w)), _dot_nt(xs, xs), att[hd])
    for hd in range(GLA_HEADS):
        cv = slice(hd * GLA_HEAD_V, (hd + 1) * GLA_HEAD_V)
        oin_out[:, cv] = _dot(att[hd].astype(BF16), vb[:, cv])


def _gla_proj(x, gain, w_in, w_gk_up, b_gk):
    s = x.shape[0]
    n = GLA_BLOCK
    nb = s // n
    main_w = 2 * GLA_DK + 2 * GLA_DV
    wmain = w_in[:, :main_w].astype(BF16)
    wa = jnp.pad(w_in[:, main_w:], ((0, 0), (0, LANES - GATE_RANK))).astype(BF16)
    wgk = jnp.pad(w_gk_up, ((0, LANES - GATE_RANK), (0, 0))).astype(BF16)
    wvt = w_in[:, 2 * GLA_DK:2 * GLA_DK + GLA_DV].T.astype(BF16)
    gmat, lvl = _gla_tables()
    row = lambda w: pl.BlockSpec((n, w), lambda i: (i, 0))
    return pl.pallas_call(
        _gla_proj_kernel,
        grid=(nb,),
        in_specs=[row(D_MODEL), _const_spec((1, D_MODEL)), _const_spec(wmain.shape), _const_spec(wa.shape),
                  _const_spec(wgk.shape), _const_spec((1, GLA_DK)), _const_spec(gmat.shape), _const_spec(lvl.shape),
                  _const_spec(wvt.shape)],
        out_specs=[row(GLA_DK), row(GLA_DK), pl.BlockSpec((1, GLA_DV, n), lambda i: (i, 0, 0)), row(GLA_DV),
                   pl.BlockSpec((1, 1, GLA_DK), lambda i: (i, 0, 0)), row(GLA_DV)],
        out_shape=[jax.ShapeDtypeStruct((s, GLA_DK), BF16), jax.ShapeDtypeStruct((s, GLA_DK), BF16),
                   jax.ShapeDtypeStruct((nb, GLA_DV, n), BF16), jax.ShapeDtypeStruct((s, GLA_DV), F32),
                   jax.ShapeDtypeStruct((nb, 1, GLA_DK), F32), jax.ShapeDtypeStruct((s, GLA_DV), F32)],
        compiler_params=_params("parallel"),
        name="gla_proj",
    )(x, gain.reshape(1, -1), wmain, wa, wgk, b_gk.reshape(1, -1), gmat, lvl, wvt)


def _gla_scan_kernel(qi_ref, kt_ref, vt_ref, dec_ref, oin_ref, o_ref, st_sc):
    @pl.when(pl.program_id(0) == 0)
    def _():
        st_sc[...] = jnp.zeros(st_sc.shape, F32)

    qi = qi_ref[...]
    kt = kt_ref[...]
    dec = dec_ref[0]
    for hd in range(GLA_HEADS):
        ck = slice(hd * GLA_HEAD_K, (hd + 1) * GLA_HEAD_K)
        cv = slice(hd * GLA_HEAD_V, (hd + 1) * GLA_HEAD_V)
        st = st_sc[hd]
        o_ref[:, cv] = oin_ref[:, cv] + _dot_nt(qi[:, ck], st.astype(BF16))
        st_sc[hd] = st * dec[:, ck] + _dot(vt_ref[0, cv, :], kt[:, ck])


def _gla_scan(qi, kt, vt, dec, oin):
    s = qi.shape[0]
    n = GLA_BLOCK
    row = lambda w: pl.BlockSpec((n, w), lambda i: (i, 0))
    return pl.pallas_call(
        _gla_scan_kernel,
        grid=(s // n,),
        in_specs=[row(GLA_DK), row(GLA_DK), pl.BlockSpec((1, GLA_DV, n), lambda i: (i, 0, 0)),
                  pl.BlockSpec((1, 1, GLA_DK), lambda i: (i, 0, 0)), row(GLA_DV)],
        out_specs=row(GLA_DV),
        out_shape=jax.ShapeDtypeStruct((s, GLA_DV), F32),
        scratch_shapes=[pltpu.VMEM((GLA_HEADS, GLA_HEAD_V, GLA_HEAD_K), F32)],
        compiler_params=_params("arbitrary"),
        name="gla_scan",
    )(qi, kt, vt, dec, oin)


def kernel(x, positions, norm_mix, norm_mlp, mla_w_in, mla_q_norm, mla_w_uq, mla_kv_norm, mla_w_ukv, mla_w_o,
           gla_w_in, gla_w_gk_up, gla_b_gk, gla_g_norm, gla_w_o, mlp_w_up, mlp_w_down, final_norm):
    b, s, d = x.shape
    outs = []
    for bi in range(b):
        xb = x[bi]
        cos_t, sin_t = _rope_tables(positions[bi])
        for i in range(DEPTH):
            j = i // 2
            fin = final_norm if i == DEPTH - 1 else None
            if i % 2 == 0:
                q, k, vt = _mla_proj(xb, norm_mix[i], mla_w_in[j], mla_q_norm[j], mla_w_uq[j], mla_kv_norm[j],
                                     mla_w_ukv[j], cos_t, sin_t)
                o = _flash(q, k, vt)
                xb = _post(xb, o, mla_w_o[j], norm_mlp[i], mlp_w_up[i], mlp_w_down[i], final_norm=fin)
            else:
                qi, kt, vt, gate, dec, oin = _gla_proj(xb, norm_mix[i], gla_w_in[j], gla_w_gk_up[j], gla_b_gk[j])
                o = _gla_scan(qi, kt, vt, dec, oin)
                xb = _post(xb, o, gla_w_o[j], norm_mlp[i], mlp_w_up[i], mlp_w_down[i], gate=gate,
                           g_norm=gla_g_norm[j], final_norm=fin)
        outs.append(xb)
    return jnp.stack(outs)
```

```python
import functools
import math

import jax
import jax.numpy as jnp
import numpy as np
from jax import lax
from jax.experimental import pallas as pl
from jax.experimental.pallas import tpu as pltpu

F32 = jnp.float32
BF16 = jnp.bfloat16

D_MODEL = 1024
DEPTH = 4
MLA_HEADS = 8
QK_NOPE = 128
QK_ROPE = 64
V_HEAD = 128
Q_LORA = 384
KV_LORA = 256
ROPE_THETA = 10000.0
GLA_HEADS = 4
GLA_DK = 512
GLA_DV = 1024
GLA_HEAD_K = 128
GLA_HEAD_V = 256
GATE_RANK = 16
GATE_NORMALIZER = 16.0
D_FF = 4096
EPS = 1e-6

LANES = 128
QK_PAD = 256
VMEM_LIMIT = 56 * 1024 * 1024

TM_PROJ = 256
TQ = 1024
QG = 256
FLASH_UNROLL = 4
FLASH_SKEW = 2
TM_POST = 512
GLA_BLOCK = 256
FF_CHUNK = 1024

NT_DIMS = (((1,), (1,)), ((), ()))
TN_DIMS = (((0,), (0,)), ((), ()))


def _dot(a, b):
    return jnp.dot(a, b, preferred_element_type=F32)


def _dot_nt(a, b):
    return lax.dot_general(a, b, NT_DIMS, preferred_element_type=F32)


def _dot_tn(a, b):
    return lax.dot_general(a, b, TN_DIMS, preferred_element_type=F32)


def _rms(x, gain):
    ms = jnp.mean(x * x, axis=-1, keepdims=True)
    return x * lax.rsqrt(ms + EPS) * gain


def _params(*sem):
    return pltpu.CompilerParams(dimension_semantics=sem, vmem_limit_bytes=VMEM_LIMIT)


def _const_spec(shape):
    nd = len(shape)
    return pl.BlockSpec(shape, lambda *_: (0,) * nd, pipeline_mode=pl.Buffered(1))


def _rope_kernel(pos_ref, invf_ref, sgn_ref, cos_ref, sin_ref):
    ang = pos_ref[...].astype(F32) * invf_ref[...]
    cos_ref[...] = jnp.cos(ang)
    sin_ref[...] = jnp.sin(ang) * sgn_ref[...]


def _rope_tables(positions):
    s = positions.shape[0]
    lane = np.arange(LANES)
    inv_freq = 1.0 / (ROPE_THETA ** (jnp.arange(0, QK_ROPE, 2, dtype=F32) / QK_ROPE))
    invf = inv_freq[lane % (QK_ROPE // 2)].reshape(1, LANES)
    sgn = jnp.asarray(np.where(lane % QK_ROPE < QK_ROPE // 2, -1.0, 1.0), F32).reshape(1, LANES)
    tm = 1024
    return pl.pallas_call(
        _rope_kernel,
        grid=(s // tm,),
        in_specs=[pl.BlockSpec((tm, 1), lambda i: (i, 0)),
                  pl.BlockSpec((1, LANES), lambda i: (0, 0)),
                  pl.BlockSpec((1, LANES), lambda i: (0, 0))],
        out_specs=[pl.BlockSpec((tm, LANES), lambda i: (i, 0))] * 2,
        out_shape=[jax.ShapeDtypeStruct((s, LANES), F32)] * 2,
        compiler_params=_params("parallel"),
        name="rope_tables",
    )(positions.reshape(s, 1), invf, sgn)


def _mla_proj_kernel(x_ref, g_ref, win_ref, qn_ref, kvn_ref, wq_ref, wk_ref, wvt_ref, cos_ref, sin_ref,
                     q_out, k_out, vt_out, *, q_scale):
    xn = _rms(x_ref[...], g_ref[...]).astype(BF16)
    h = _dot(xn, win_ref[...])
    cq = _rms(h[:, :Q_LORA], qn_ref[...]).astype(BF16)
    ckv = _rms(h[:, Q_LORA:Q_LORA + KV_LORA], kvn_ref[...]).astype(BF16)
    cosv = cos_ref[...]
    sinv = sin_ref[...]
    r0 = Q_LORA + KV_LORA
    k_rot = (h[:, r0:r0 + LANES] * cosv + h[:, r0 + LANES:r0 + 2 * LANES] * sinv).astype(BF16)
    qb = _dot(cq, wq_ref[...])
    kn = _dot(ckv, wk_ref[...])
    hw = MLA_HEADS * LANES
    for hd in range(MLA_HEADS):
        c = slice(hd * LANES, (hd + 1) * LANES)
        q_out[hd, :, 0:LANES] = (qb[:, c] * q_scale).astype(BF16)
        q_rot = qb[:, hw + hd * LANES:hw + (hd + 1) * LANES] * cosv \
            + qb[:, 2 * hw + hd * LANES:2 * hw + (hd + 1) * LANES] * sinv
        q_out[hd, :, LANES:QK_PAD] = (q_rot * q_scale).astype(BF16)
        k_out[hd, :, 0:LANES] = kn[:, c].astype(BF16)
        k_out[hd, :, LANES:QK_PAD] = k_rot
    vt_out[0] = _dot_nt(wvt_ref[...], ckv).astype(BF16)


def _pad_rope_cols(w, n_heads):
    k = w.shape[0]
    w = w.reshape(k, n_heads, QK_ROPE)
    half = QK_ROPE // 2
    sw = jnp.concatenate([w[..., half:], w[..., :half]], axis=-1)
    z = jnp.zeros((k, n_heads, LANES - QK_ROPE), w.dtype)
    plain = jnp.concatenate([w, z], axis=-1).reshape(k, n_heads * LANES)
    swapped = jnp.concatenate([sw, z], axis=-1).reshape(k, n_heads * LANES)
    return plain, swapped


def _mla_proj(x, gain, w_in, q_norm, w_uq, kv_norm, w_ukv, cos_t, sin_t):
    s = x.shape[0]
    tm = TM_PROJ
    r0 = Q_LORA + KV_LORA
    kr, kr_sw = _pad_rope_cols(w_in[:, r0:], 1)
    win = jnp.concatenate([w_in[:, :r0], kr, kr_sw], axis=1).astype(BF16)
    wq3 = w_uq.reshape(Q_LORA, MLA_HEADS, QK_NOPE + QK_ROPE)
    qr, qr_sw = _pad_rope_cols(wq3[..., QK_NOPE:].reshape(Q_LORA, MLA_HEADS * QK_ROPE), MLA_HEADS)
    wq = jnp.concatenate([wq3[..., :QK_NOPE].reshape(Q_LORA, -1), qr, qr_sw], axis=1).astype(BF16)
    wkv3 = w_ukv.reshape(KV_LORA, MLA_HEADS, QK_NOPE + V_HEAD)
    wk = wkv3[..., :QK_NOPE].reshape(KV_LORA, -1).astype(BF16)
    wvt = wkv3[..., QK_NOPE:].reshape(KV_LORA, -1).T.astype(BF16)
    q_scale = (QK_NOPE + QK_ROPE) ** -0.5 * math.log2(math.e)
    nt = s // tm
    return pl.pallas_call(
        functools.partial(_mla_proj_kernel, q_scale=q_scale),
        grid=(nt,),
        in_specs=[pl.BlockSpec((tm, D_MODEL), lambda i: (i, 0)),
                  _const_spec((1, D_MODEL)),
                  _const_spec(win.shape),
                  _const_spec((1, Q_LORA)),
                  _const_spec((1, KV_LORA)),
                  _const_spec(wq.shape),
                  _const_spec(wk.shape),
                  _const_spec(wvt.shape),
                  pl.BlockSpec((tm, LANES), lambda i: (i, 0)),
                  pl.BlockSpec((tm, LANES), lambda i: (i, 0))],
        out_specs=[pl.BlockSpec((MLA_HEADS, tm, QK_PAD), lambda i: (0, i, 0)),
                   pl.BlockSpec((MLA_HEADS, tm, QK_PAD), lambda i: (0, i, 0)),
                   pl.BlockSpec((1, MLA_HEADS * V_HEAD, tm), lambda i: (i, 0, 0))],
        out_shape=[jax.ShapeDtypeStruct((MLA_HEADS, s, QK_PAD), BF16),
                   jax.ShapeDtypeStruct((MLA_HEADS, s, QK_PAD), BF16),
                   jax.ShapeDtypeStruct((nt, MLA_HEADS * V_HEAD, tm), BF16)],
        compiler_params=_params("parallel"),
        name="mla_proj",
    )(x, gain.reshape(1, -1), win, q_norm.reshape(1, -1), kv_norm.reshape(1, -1), wq, wk, wvt, cos_t, sin_t)


def _flash_kernel(q_ref, k_ref, vt_ref, o_ref, m_sc, l_sc, acc_sc, s_sc, cm_sc):
    qi = pl.program_id(1)
    tk = TM_PROJ
    n_sub = TQ // tk
    n_grp = TQ // QG
    m_sc[...] = jnp.full(m_sc.shape, -jnp.inf, F32)
    l_sc[...] = jnp.zeros(l_sc.shape, F32)
    acc_sc[...] = jnp.zeros(acc_sc.shape, F32)

    def unit(g, diag):
        cols = slice(g * QG, (g + 1) * QG)
        trimmed = diag and g < n_grp - FLASH_SKEW
        rows = ((g * QG) // tk + 1) * tk if trimmed else TQ
        return cols, rows

    def scores(kstep, g, diag):
        cols, rows = unit(g, diag)
        start = pl.multiple_of(kstep * TQ, TQ)
        st = _dot_nt(k_ref[0, pl.ds(start, rows), :], q_ref[0, cols, :])
        if diag:
            kidx = lax.broadcasted_iota(jnp.int32, st.shape, 0)
            qidx = cols.start + lax.broadcasted_iota(jnp.int32, st.shape, 1)
            st = jnp.where(kidx <= qidx, st, -jnp.inf)
        s_sc[g, 0:rows, :] = st
        cm_sc[g] = jnp.max(st, axis=0, keepdims=True)

    def consume(kstep, g, diag):
        cols, rows = unit(g, diag)
        m_old = m_sc[g]
        m_new = jnp.maximum(m_old, cm_sc[g])
        alpha = jnp.exp2(m_old - m_new)
        p = jnp.exp2(s_sc[g, 0:rows, :] - m_new)
        l_sc[g] = alpha * l_sc[g] + jnp.sum(p, axis=0, keepdims=True)
        pb = p.astype(BF16)
        pv = _dot(vt_ref[kstep * n_sub], pb[0:tk])
        for jj in range(1, rows // tk):
            pv = pv + _dot(vt_ref[kstep * n_sub + jj], pb[jj * tk:(jj + 1) * tk])
        acc_sc[g] = alpha * acc_sc[g] + pv
        m_sc[g] = m_new

    for g in range(n_grp):
        scores(qi, g, True)
        if g >= FLASH_SKEW:
            consume(qi, g - FLASH_SKEW, True)

    def full_steps(first, count):
        for n in [first + d for d in range(count)]:
            prev = jnp.where(n == 0, qi, n - 1)
            for g in range(n_grp):
                scores(n, g, False)
                if g >= FLASH_SKEW:
                    consume(n, g - FLASH_SKEW, False)
                else:
                    consume(prev, g - FLASH_SKEW + n_grp, False)

    def body(t, carry):
        full_steps(t * FLASH_UNROLL, FLASH_UNROLL)
        return carry

    lax.fori_loop(0, qi // FLASH_UNROLL, body, 0)
    done = (qi // FLASH_UNROLL) * FLASH_UNROLL
    part = FLASH_UNROLL // 2
    while part >= 1:
        pl.when((qi & part) != 0)(functools.partial(full_steps, done, part))
        done = done + (qi & part)
        part //= 2
    for g in range(n_grp - FLASH_SKEW, n_grp):
        consume(jnp.where(qi == 0, qi, qi - 1), g, False)
    for g in range(n_grp):
        o = acc_sc[g] * (1.0 / l_sc[g])
        o_ref[g * QG:(g + 1) * QG, :] = o.T.astype(BF16)


def _flash(q, k, vt):
    s = q.shape[1]
    tk = TM_PROJ
    nkb = s // tk
    n_grp = TQ // QG
    return pl.pallas_call(
        _flash_kernel,
        grid=(MLA_HEADS, s // TQ),
        in_specs=[pl.BlockSpec((1, TQ, QK_PAD), lambda h, i: (h, i, 0)),
                  pl.BlockSpec((1, s, QK_PAD), lambda h, i: (h, 0, 0)),
                  pl.BlockSpec((nkb, V_HEAD, tk), lambda h, i: (0, h, 0))],
        out_specs=pl.BlockSpec((TQ, V_HEAD), lambda h, i: (i, h)),
        out_shape=jax.ShapeDtypeStruct((s, MLA_HEADS * V_HEAD), BF16),
        scratch_shapes=[pltpu.VMEM((n_grp, 1, QG), F32), pltpu.VMEM((n_grp, 1, QG), F32),
                        pltpu.VMEM((n_grp, V_HEAD, QG), F32), pltpu.VMEM((n_grp, TQ, QG), F32),
                        pltpu.VMEM((n_grp, 1, QG), F32)],
        compiler_params=_params("parallel", "arbitrary"),
        name="mla_flash",
    )(q, k, vt)


def _post_kernel(*refs, gla, final):
    if gla:
        x_ref, o_ref, gt_ref, gn_ref = refs[:4]
        refs = refs[4:]
    else:
        x_ref, o_ref = refs[:2]
        refs = refs[2:]
    wo_ref, nm_ref, wup_ref, wdn_ref = refs[:4]
    fn_ref = refs[4] if final else None
    out_ref = refs[-1]

    if gla:
        o = o_ref[...]
        gt = gt_ref[...]
        gn = gn_ref[...]
        parts = []
        for hd in range(GLA_HEADS):
            c = slice(hd * GLA_HEAD_V, (hd + 1) * GLA_HEAD_V)
            g = gt[:, c]
            parts.append((_rms(o[:, c], gn) * (g * (1.0 / (1.0 + jnp.exp(-g))))).astype(BF16))
        ob = jnp.concatenate(parts, axis=1)
    else:
        ob = o_ref[...]
    x1 = x_ref[...] + _dot(ob, wo_ref[...])
    xn = _rms(x1, nm_ref[...]).astype(BF16)
    y = jnp.zeros_like(x1)
    for c in range(D_FF // FF_CHUNK):
        hid = _dot(xn, wup_ref[:, c * FF_CHUNK:(c + 1) * FF_CHUNK])
        hid = jnp.square(jnp.maximum(hid, 0.0)).astype(BF16)
        y = y + _dot(hid, wdn_ref[c * FF_CHUNK:(c + 1) * FF_CHUNK, :])
    x2 = x1 + y
    if final:
        x2 = _rms(x2, fn_ref[...])
    out_ref[...] = x2


def _post(x, o, w_o, norm_mlp, w_up, w_down, *, gate=None, g_norm=None, final_norm=None):
    s = x.shape[0]
    tm = TM_POST
    gla = gate is not None
    final = final_norm is not None
    row = lambda w: pl.BlockSpec((tm, w), lambda i: (i, 0))
    args = [x, o]
    specs = [row(D_MODEL), row(D_MODEL)]
    if gla:
        args += [gate, g_norm.reshape(1, -1)]
        specs += [row(GLA_DV), _const_spec((1, GLA_HEAD_V))]
    args += [w_o.astype(BF16), norm_mlp.reshape(1, -1), w_up.astype(BF16), w_down.astype(BF16)]
    specs += [_const_spec(w_o.shape), _const_spec((1, D_MODEL)), _const_spec(w_up.shape), _const_spec(w_down.shape)]
    if final:
        args.append(final_norm.reshape(1, -1))
        specs.append(_const_spec((1, D_MODEL)))
    return pl.pallas_call(
        functools.partial(_post_kernel, gla=gla, final=final),
        grid=(s // tm,),
        in_specs=specs,
        out_specs=row(D_MODEL),
        out_shape=jax.ShapeDtypeStruct((s, D_MODEL), F32),
        compiler_params=_params("parallel"),
        name="post_gla" if gla else "post_mla",
    )(*args)


GLA_LEVELS = tuple(GLA_BLOCK >> (i + 1) for i in range(int(math.log2(GLA_BLOCK))))
SUBLANES = 8
GLA_MATMUL_LEVELS = tuple(w for w in GLA_LEVELS if 2 * w < SUBLANES)


def _gla_tables():
    n = GLA_BLOCK
    t = np.arange(n)
    u = np.arange(n)[None, :]
    mats = [(u <= t[:, None])]
    for w in GLA_MATMUL_LEVELS:
        m = (t // (2 * w)) * (2 * w) + w
        upper = (t % (2 * w)) >= w
        g = np.where(upper[:, None], (u > m[:, None]) & (u <= t[:, None]),
                     (u > t[:, None]) & (u <= m[:, None]))
        mats.append(g)
    gmat = np.stack(mats).astype(np.float32)
    x = t[:, None] ^ t[None, :]
    lvl = np.where(t[:, None] > t[None, :], np.floor(np.log2(np.maximum(x, 1))), np.where(x == 0, -1, -2))
    return jnp.asarray(gmat, BF16), jnp.asarray(lvl, jnp.int32)


def _gla_proj_kernel(x_ref, g_ref, wmain_ref, wa_ref, wgk_ref, bgk_ref, gmat_ref, lvl_ref,
                     qi_out, kt_out, v_out, gate_out, dec_out, oin_out):
    n = GLA_BLOCK
    dk = GLA_DK
    xn = _rms(x_ref[...], g_ref[...]).astype(BF16)
    hm = _dot(xn, wmain_ref[...])
    a = _dot(xn, wa_ref[...])
    z = _dot(a.astype(BF16), wgk_ref[...]) + bgk_ref[...]
    gk = (jnp.minimum(z, 0.0) - jnp.log(1.0 + jnp.exp(-jnp.abs(z)))) * (1.0 / GATE_NORMALIZER)
    hi = gk.astype(BF16)
    lo = (gk - hi.astype(F32)).astype(BF16)
    hilo = jnp.concatenate([hi, lo], axis=1)

    def decay_sum(idx):
        e = _dot(gmat_ref[idx], hilo)
        return e[:, :dk] + e[:, dk:]

    b = decay_sum(0)
    b_last = b[n - 1:n, :]
    dec_out[0] = jnp.exp(b_last)
    q = hm[:, :dk] * (GLA_HEAD_K ** -0.5)
    k = hm[:, dk:2 * dk]
    vb = hm[:, 2 * dk:2 * dk + GLA_DV].astype(BF16)
    qi_out[...] = (q * jnp.exp(b)).astype(BF16)
    kt_out[...] = (k * jnp.exp(b_last - b)).astype(BF16)
    gate_out[...] = hm[:, 2 * dk + GLA_DV:]
    v_out[...] = vb

    lvl = lvl_ref[...]
    row = lax.broadcasted_iota(jnp.int32, (n, GLA_HEAD_K), 0)
    qb = q.astype(BF16)
    kb = k.astype(BF16)
    att = []
    for hd in range(GLA_HEADS):
        c = slice(hd * GLA_HEAD_K, (hd + 1) * GLA_HEAD_K)
        att.append(jnp.where(lvl == -1, _dot_nt(qb[:, c], kb[:, c]), 0.0))
    for w in GLA_LEVELS:
        seg = 2 * w
        if w in GLA_MATMUL_LEVELS:
            e = jnp.exp(decay_sum(1 + GLA_MATMUL_LEVELS.index(w)))
        else:
            b3 = b.reshape(n // seg, seg, dk)
            mid = jnp.broadcast_to(b3[:, w:w + 1, :], b3.shape).reshape(n, dk)
            e = jnp.exp(-jnp.abs(b - mid))
        upper = (row & (seg - 1)) >= w
        for hd in range(GLA_HEADS):
            c = slice(hd * GLA_HEAD_K, (hd + 1) * GLA_HEAD_K)
            xs = (jnp.where(upper, q[:, c], k[:, c]) * e[:, c]).astype(BF16)
            att[hd] = jnp.where(lvl == int(math.log2(w)), _dot_nt(xs, xs), att[hd])
    for hd in range(GLA_HEADS):
        cv = slice(hd * GLA_HEAD_V, (hd + 1) * GLA_HEAD_V)
        oin_out[:, cv] = _dot(att[hd].astype(BF16), vb[:, cv])


def _gla_proj(x, gain, w_in, w_gk_up, b_gk):
    s = x.shape[0]
    n = GLA_BLOCK
    nb = s // n
    main_w = 2 * GLA_DK + 2 * GLA_DV
    wmain = w_in[:, :main_w].astype(BF16)
    wa = jnp.pad(w_in[:, main_w:], ((0, 0), (0, LANES - GATE_RANK))).astype(BF16)
    wgk = jnp.pad(w_gk_up, ((0, LANES - GATE_RANK), (0, 0))).astype(BF16)
    gmat, lvl = _gla_tables()
    row = lambda w: pl.BlockSpec((n, w), lambda i: (i, 0))
    return pl.pallas_call(
        _gla_proj_kernel,
        grid=(nb,),
        in_specs=[row(D_MODEL), _const_spec((1, D_MODEL)), _const_spec(wmain.shape), _const_spec(wa.shape),
                  _const_spec(wgk.shape), _const_spec((1, GLA_DK)), _const_spec(gmat.shape), _const_spec(lvl.shape)],
        out_specs=[row(GLA_DK), row(GLA_DK), row(GLA_DV), row(GLA_DV),
                   pl.BlockSpec((1, 1, GLA_DK), lambda i: (i, 0, 0)), row(GLA_DV)],
        out_shape=[jax.ShapeDtypeStruct((s, GLA_DK), BF16), jax.ShapeDtypeStruct((s, GLA_DK), BF16),
                   jax.ShapeDtypeStruct((s, GLA_DV), BF16), jax.ShapeDtypeStruct((s, GLA_DV), F32),
                   jax.ShapeDtypeStruct((nb, 1, GLA_DK), F32), jax.ShapeDtypeStruct((s, GLA_DV), F32)],
        compiler_params=_params("parallel"),
        name="gla_proj",
    )(x, gain.reshape(1, -1), wmain, wa, wgk, b_gk.reshape(1, -1), gmat, lvl)


def _gla_scan_kernel(qi_ref, kt_ref, v_ref, dec_ref, oin_ref, o_ref, st_sc):
    @pl.when(pl.program_id(0) == 0)
    def _():
        st_sc[...] = jnp.zeros(st_sc.shape, F32)

    qi = qi_ref[...]
    kt = kt_ref[...]
    dec = dec_ref[0]
    for hd in range(GLA_HEADS):
        ck = slice(hd * GLA_HEAD_K, (hd + 1) * GLA_HEAD_K)
        cv = slice(hd * GLA_HEAD_V, (hd + 1) * GLA_HEAD_V)
        st = st_sc[hd]
        o_ref[:, cv] = oin_ref[:, cv] + _dot_nt(qi[:, ck], st.astype(BF16))
        st_sc[hd] = st * dec[:, ck] + _dot_tn(v_ref[:, cv], kt[:, ck])


def _gla_scan(qi, kt, v, dec, oin):
    s = qi.shape[0]
    n = GLA_BLOCK
    row = lambda w: pl.BlockSpec((n, w), lambda i: (i, 0))
    return pl.pallas_call(
        _gla_scan_kernel,
        grid=(s // n,),
        in_specs=[row(GLA_DK), row(GLA_DK), row(GLA_DV),
                  pl.BlockSpec((1, 1, GLA_DK), lambda i: (i, 0, 0)), row(GLA_DV)],
        out_specs=row(GLA_DV),
        out_shape=jax.ShapeDtypeStruct((s, GLA_DV), F32),
        scratch_shapes=[pltpu.VMEM((GLA_HEADS, GLA_HEAD_V, GLA_HEAD_K), F32)],
        compiler_params=_params("arbitrary"),
        name="gla_scan",
    )(qi, kt, v, dec, oin)


def kernel(x, positions, norm_mix, norm_mlp, mla_w_in, mla_q_norm, mla_w_uq, mla_kv_norm, mla_w_ukv, mla_w_o,
           gla_w_in, gla_w_gk_up, gla_b_gk, gla_g_norm, gla_w_o, mlp_w_up, mlp_w_down, final_norm):
    b, s, d = x.shape
    outs = []
    for bi in range(b):
        xb = x[bi]
        cos_t, sin_t = _rope_tables(positions[bi])
        for i in range(DEPTH):
            j = i // 2
            fin = final_norm if i == DEPTH - 1 else None
            if i % 2 == 0:
                q, k, vt = _mla_proj(xb, norm_mix[i], mla_w_in[j], mla_q_norm[j], mla_w_uq[j], mla_kv_norm[j],
                                     mla_w_ukv[j], cos_t, sin_t)
                o = _flash(q, k, vt)
                xb = _post(xb, o, mla_w_o[j], norm_mlp[i], mlp_w_up[i], mlp_w_down[i], final_norm=fin)
            else:
                qi, kt, v, gate, dec, oin = _gla_proj(xb, norm_mix[i], gla_w_in[j], gla_w_gk_up[j], gla_b_gk[j])
                o = _gla_scan(qi, kt, v, dec, oin)
                xb = _post(xb, o, gla_w_o[j], norm_mlp[i], mlp_w_up[i], mlp_w_down[i], gate=gate,
                           g_norm=gla_g_norm[j], final_norm=fin)
        outs.append(xb)
    return jnp.stack(outs)
```

```python
import functools
import math

import jax
import jax.numpy as jnp
import numpy as np
from jax import lax
from jax.experimental import pallas as pl
from jax.experimental.pallas import tpu as pltpu

F32 = jnp.float32
BF16 = jnp.bfloat16

D_MODEL = 1024
DEPTH = 4
MLA_HEADS = 8
QK_NOPE = 128
QK_ROPE = 64
V_HEAD = 128
Q_LORA = 384
KV_LORA = 256
ROPE_THETA = 10000.0
GLA_HEADS = 4
GLA_DK = 512
GLA_DV = 1024
GLA_HEAD_K = 128
GLA_HEAD_V = 256
GATE_RANK = 16
GATE_NORMALIZER = 16.0
D_FF = 4096
EPS = 1e-6

LANES = 128
QK_PAD = 256
VMEM_LIMIT = 56 * 1024 * 1024

TM_PROJ = 256
TQ = 1024
QG = 256
FLASH_UNROLL = 4
FLASH_SKEW = 2
TM_POST = 512
GLA_BLOCK = 256
FF_CHUNK = 1024

NT_DIMS = (((1,), (1,)), ((), ()))
TN_DIMS = (((0,), (0,)), ((), ()))


def _dot(a, b):
    return jnp.dot(a, b, preferred_element_type=F32)


def _dot_nt(a, b):
    return lax.dot_general(a, b, NT_DIMS, preferred_element_type=F32)


def _dot_tn(a, b):
    return lax.dot_general(a, b, TN_DIMS, preferred_element_type=F32)


def _rms(x, gain):
    ms = jnp.mean(x * x, axis=-1, keepdims=True)
    return x * lax.rsqrt(ms + EPS) * gain


def _params(*sem):
    return pltpu.CompilerParams(dimension_semantics=sem, vmem_limit_bytes=VMEM_LIMIT)


def _const_spec(shape):
    nd = len(shape)
    return pl.BlockSpec(shape, lambda *_: (0,) * nd, pipeline_mode=pl.Buffered(1))


def _rope_kernel(pos_ref, invf_ref, sgn_ref, cos_ref, sin_ref):
    ang = pos_ref[...].astype(F32) * invf_ref[...]
    cos_ref[...] = jnp.cos(ang)
    sin_ref[...] = jnp.sin(ang) * sgn_ref[...]


def _rope_tables(positions):
    s = positions.shape[0]
    lane = np.arange(LANES)
    inv_freq = 1.0 / (ROPE_THETA ** (jnp.arange(0, QK_ROPE, 2, dtype=F32) / QK_ROPE))
    invf = inv_freq[lane % (QK_ROPE // 2)].reshape(1, LANES)
    sgn = jnp.asarray(np.where(lane % QK_ROPE < QK_ROPE // 2, -1.0, 1.0), F32).reshape(1, LANES)
    tm = 1024
    return pl.pallas_call(
        _rope_kernel,
        grid=(s // tm,),
        in_specs=[pl.BlockSpec((tm, 1), lambda i: (i, 0)),
                  pl.BlockSpec((1, LANES), lambda i: (0, 0)),
                  pl.BlockSpec((1, LANES), lambda i: (0, 0))],
        out_specs=[pl.BlockSpec((tm, LANES), lambda i: (i, 0))] * 2,
        out_shape=[jax.ShapeDtypeStruct((s, LANES), F32)] * 2,
        compiler_params=_params("parallel"),
        name="rope_tables",
    )(positions.reshape(s, 1), invf, sgn)


def _mla_proj_kernel(x_ref, g_ref, win_ref, qn_ref, kvn_ref, wq_ref, wk_ref, wvt_ref, cos_ref, sin_ref,
                     q_out, k_out, vt_out, *, q_scale):
    xn = _rms(x_ref[...], g_ref[...]).astype(BF16)
    h = _dot(xn, win_ref[...])
    cq = _rms(h[:, :Q_LORA], qn_ref[...]).astype(BF16)
    ckv = _rms(h[:, Q_LORA:Q_LORA + KV_LORA], kvn_ref[...]).astype(BF16)
    cosv = cos_ref[...]
    sinv = sin_ref[...]
    r0 = Q_LORA + KV_LORA
    k_rot = (h[:, r0:r0 + LANES] * cosv + h[:, r0 + LANES:r0 + 2 * LANES] * sinv).astype(BF16)
    qb = _dot(cq, wq_ref[...])
    kn = _dot(ckv, wk_ref[...])
    hw = MLA_HEADS * LANES
    for hd in range(MLA_HEADS):
        c = slice(hd * LANES, (hd + 1) * LANES)
        q_out[hd, :, 0:LANES] = (qb[:, c] * q_scale).astype(BF16)
        q_rot = qb[:, hw + hd * LANES:hw + (hd + 1) * LANES] * cosv \
            + qb[:, 2 * hw + hd * LANES:2 * hw + (hd + 1) * LANES] * sinv
        q_out[hd, :, LANES:QK_PAD] = (q_rot * q_scale).astype(BF16)
        k_out[hd, :, 0:LANES] = kn[:, c].astype(BF16)
        k_out[hd, :, LANES:QK_PAD] = k_rot
    vt_out[0] = _dot_nt(wvt_ref[...], ckv).astype(BF16)


def _pad_rope_cols(w, n_heads):
    k = w.shape[0]
    w = w.reshape(k, n_heads, QK_ROPE)
    half = QK_ROPE // 2
    sw = jnp.concatenate([w[..., half:], w[..., :half]], axis=-1)
    z = jnp.zeros((k, n_heads, LANES - QK_ROPE), w.dtype)
    plain = jnp.concatenate([w, z], axis=-1).reshape(k, n_heads * LANES)
    swapped = jnp.concatenate([sw, z], axis=-1).reshape(k, n_heads * LANES)
    return plain, swapped


def _mla_proj(x, gain, w_in, q_norm, w_uq, kv_norm, w_ukv, cos_t, sin_t):
    s = x.shape[0]
    tm = TM_PROJ
    r0 = Q_LORA + KV_LORA
    kr, kr_sw = _pad_rope_cols(w_in[:, r0:], 1)
    win = jnp.concatenate([w_in[:, :r0], kr, kr_sw], axis=1).astype(BF16)
    wq3 = w_uq.reshape(Q_LORA, MLA_HEADS, QK_NOPE + QK_ROPE)
    qr, qr_sw = _pad_rope_cols(wq3[..., QK_NOPE:].reshape(Q_LORA, MLA_HEADS * QK_ROPE), MLA_HEADS)
    wq = jnp.concatenate([wq3[..., :QK_NOPE].reshape(Q_LORA, -1), qr, qr_sw], axis=1).astype(BF16)
    wkv3 = w_ukv.reshape(KV_LORA, MLA_HEADS, QK_NOPE + V_HEAD)
    wk = wkv3[..., :QK_NOPE].reshape(KV_LORA, -1).astype(BF16)
    wvt = wkv3[..., QK_NOPE:].reshape(KV_LORA, -1).T.astype(BF16)
    q_scale = (QK_NOPE + QK_ROPE) ** -0.5 * math.log2(math.e)
    nt = s // tm
    return pl.pallas_call(
        functools.partial(_mla_proj_kernel, q_scale=q_scale),
        grid=(nt,),
        in_specs=[pl.BlockSpec((tm, D_MODEL), lambda i: (i, 0)),
                  _const_spec((1, D_MODEL)),
                  _const_spec(win.shape),
                  _const_spec((1, Q_LORA)),
                  _const_spec((1, KV_LORA)),
                  _const_spec(wq.shape),
                  _const_spec(wk.shape),
                  _const_spec(wvt.shape),
                  pl.BlockSpec((tm, LANES), lambda i: (i, 0)),
                  pl.BlockSpec((tm, LANES), lambda i: (i, 0))],
        out_specs=[pl.BlockSpec((MLA_HEADS, tm, QK_PAD), lambda i: (0, i, 0)),
                   pl.BlockSpec((MLA_HEADS, tm, QK_PAD), lambda i: (0, i, 0)),
                   pl.BlockSpec((1, MLA_HEADS * V_HEAD, tm), lambda i: (i, 0, 0))],
        out_shape=[jax.ShapeDtypeStruct((MLA_HEADS, s, QK_PAD), BF16),
                   jax.ShapeDtypeStruct((MLA_HEADS, s, QK_PAD), BF16),
                   jax.ShapeDtypeStruct((nt, MLA_HEADS * V_HEAD, tm), BF16)],
        compiler_params=_params("parallel"),
        name="mla_proj",
    )(x, gain.reshape(1, -1), win, q_norm.reshape(1, -1), kv_norm.reshape(1, -1), wq, wk, wvt, cos_t, sin_t)


def _flash_kernel(q_ref, k_ref, vt_ref, o_ref, m_sc, l_sc, acc_sc, s_sc, cm_sc):
    qi = pl.program_id(1)
    tk = TM_PROJ
    n_sub = TQ // tk
    n_grp = TQ // QG
    m_sc[...] = jnp.full(m_sc.shape, -jnp.inf, F32)
    l_sc[...] = jnp.zeros(l_sc.shape, F32)
    acc_sc[...] = jnp.zeros(acc_sc.shape, F32)

    def unit(g, diag):
        cols = slice(g * QG, (g + 1) * QG)
        trimmed = diag and g < n_grp - FLASH_SKEW
        rows = ((g * QG) // tk + 1) * tk if trimmed else TQ
        return cols, rows

    def scores(kstep, g, diag):
        cols, rows = unit(g, diag)
        start = pl.multiple_of(kstep * TQ, TQ)
        st = _dot_nt(k_ref[0, pl.ds(start, rows), :], q_ref[0, cols, :])
        if diag:
            kidx = lax.broadcasted_iota(jnp.int32, st.shape, 0)
            qidx = cols.start + lax.broadcasted_iota(jnp.int32, st.shape, 1)
            st = jnp.where(kidx <= qidx, st, -jnp.inf)
        s_sc[g, 0:rows, :] = st
        cm_sc[g] = jnp.max(st, axis=0, keepdims=True)

    def consume(kstep, g, diag):
        cols, rows = unit(g, diag)
        m_old = m_sc[g]
        m_new = jnp.maximum(m_old, cm_sc[g])
        alpha = jnp.exp2(m_old - m_new)
        p = jnp.exp2(s_sc[g, 0:rows, :] - m_new)
        l_sc[g] = alpha * l_sc[g] + jnp.sum(p, axis=0, keepdims=True)
        pb = p.astype(BF16)
        pv = _dot(vt_ref[kstep * n_sub], pb[0:tk])
        for jj in range(1, rows // tk):
            pv = pv + _dot(vt_ref[kstep * n_sub + jj], pb[jj * tk:(jj + 1) * tk])
        acc_sc[g] = alpha * acc_sc[g] + pv
        m_sc[g] = m_new

    for g in range(n_grp):
        scores(qi, g, True)
        if g >= FLASH_SKEW:
            consume(qi, g - FLASH_SKEW, True)

    def full_steps(first, count):
        for n in [first + d for d in range(count)]:
            prev = jnp.where(n == 0, qi, n - 1)
            for g in range(n_grp):
                scores(n, g, False)
                if g >= FLASH_SKEW:
                    consume(n, g - FLASH_SKEW, False)
                else:
                    consume(prev, g - FLASH_SKEW + n_grp, False)

    def body(t, carry):
        full_steps(t * FLASH_UNROLL, FLASH_UNROLL)
        return carry

    lax.fori_loop(0, qi // FLASH_UNROLL, body, 0)
    done = (qi // FLASH_UNROLL) * FLASH_UNROLL
    part = FLASH_UNROLL // 2
    while part >= 1:
        pl.when((qi & part) != 0)(functools.partial(full_steps, done, part))
        done = done + (qi & part)
        part //= 2
    for g in range(n_grp - FLASH_SKEW, n_grp):
        consume(jnp.where(qi == 0, qi, qi - 1), g, False)
    for g in range(n_grp):
        o = acc_sc[g] * (1.0 / l_sc[g])
        o_ref[g * QG:(g + 1) * QG, :] = o.T.astype(BF16)


def _flash(q, k, vt):
    s = q.shape[1]
    tk = TM_PROJ
    nkb = s // tk
    n_grp = TQ // QG
    return pl.pallas_call(
        _flash_kernel,
        grid=(MLA_HEADS, s // TQ),
        in_specs=[pl.BlockSpec((1, TQ, QK_PAD), lambda h, i: (h, i, 0)),
                  pl.BlockSpec((1, s, QK_PAD), lambda h, i: (h, 0, 0)),
                  pl.BlockSpec((nkb, V_HEAD, tk), lambda h, i: (0, h, 0))],
        out_specs=pl.BlockSpec((TQ, V_HEAD), lambda h, i: (i, h)),
        out_shape=jax.ShapeDtypeStruct((s, MLA_HEADS * V_HEAD), BF16),
        scratch_shapes=[pltpu.VMEM((n_grp, 1, QG), F32), pltpu.VMEM((n_grp, 1, QG), F32),
                        pltpu.VMEM((n_grp, V_HEAD, QG), F32), pltpu.VMEM((n_grp, TQ, QG), F32),
                        pltpu.VMEM((n_grp, 1, QG), F32)],
        compiler_params=_params("parallel", "arbitrary"),
        name="mla_flash",
    )(q, k, vt)


def _post_kernel(*refs, gla, final):
    if gla:
        x_ref, oin_ref, gt_ref, gn_ref, qi_ref, kt_ref, v_ref, dec_ref = refs[:8]
        refs = refs[8:]
        st_sc, o_sc = refs[-2:]
        refs = refs[:-2]
    else:
        x_ref, o_ref = refs[:2]
        refs = refs[2:]
    wo_ref, nm_ref, wup_ref, wdn_ref = refs[:4]
    fn_ref = refs[4] if final else None
    out_ref = refs[-1]

    if gla:
        @pl.when(pl.program_id(0) == 0)
        def _():
            st_sc[...] = jnp.zeros(st_sc.shape, F32)

        for blk in range(TM_POST // GLA_BLOCK):
            r = slice(blk * GLA_BLOCK, (blk + 1) * GLA_BLOCK)
            dec = dec_ref[blk]
            for hd in range(GLA_HEADS):
                ck = slice(hd * GLA_HEAD_K, (hd + 1) * GLA_HEAD_K)
                cv = slice(hd * GLA_HEAD_V, (hd + 1) * GLA_HEAD_V)
                st = st_sc[hd]
                o_sc[r, cv] = oin_ref[r, cv] + _dot_nt(qi_ref[r, ck], st.astype(BF16))
                st_sc[hd] = st * dec[:, ck] + _dot_tn(v_ref[r, cv], kt_ref[r, ck])
        gn = gn_ref[...]
        parts = []
        for hd in range(GLA_HEADS):
            c = slice(hd * GLA_HEAD_V, (hd + 1) * GLA_HEAD_V)
            g = gt_ref[:, c]
            parts.append((_rms(o_sc[:, c], gn) * (g * (1.0 / (1.0 + jnp.exp(-g))))).astype(BF16))
        ob = jnp.concatenate(parts, axis=1)
    else:
        ob = o_ref[...]
    x1 = x_ref[...] + _dot(ob, wo_ref[...])
    xn = _rms(x1, nm_ref[...]).astype(BF16)
    y = jnp.zeros_like(x1)
    for c in range(D_FF // FF_CHUNK):
        hid = _dot(xn, wup_ref[:, c * FF_CHUNK:(c + 1) * FF_CHUNK])
        hid = jnp.square(jnp.maximum(hid, 0.0)).astype(BF16)
        y = y + _dot(hid, wdn_ref[c * FF_CHUNK:(c + 1) * FF_CHUNK, :])
    x2 = x1 + y
    if final:
        x2 = _rms(x2, fn_ref[...])
    out_ref[...] = x2


def _post(x, o, w_o, norm_mlp, w_up, w_down, *, gla_parts=None, g_norm=None, final_norm=None):
    s = x.shape[0]
    tm = TM_POST
    gla = gla_parts is not None
    final = final_norm is not None
    row = lambda w: pl.BlockSpec((tm, w), lambda i: (i, 0))
    args = [x, o]
    specs = [row(D_MODEL), row(D_MODEL)]
    scratch = []
    if gla:
        gate, qi, kt, v, dec = gla_parts
        nblk = tm // GLA_BLOCK
        args += [gate, g_norm.reshape(1, -1), qi, kt, v, dec]
        specs += [row(GLA_DV), _const_spec((1, GLA_HEAD_V)), row(GLA_DK), row(GLA_DK), row(GLA_DV),
                  pl.BlockSpec((nblk, 1, GLA_DK), lambda i: (i, 0, 0))]
        scratch = [pltpu.VMEM((GLA_HEADS, GLA_HEAD_V, GLA_HEAD_K), F32), pltpu.VMEM((tm, GLA_DV), F32)]
    args += [w_o.astype(BF16), norm_mlp.reshape(1, -1), w_up.astype(BF16), w_down.astype(BF16)]
    specs += [_const_spec(w_o.shape), _const_spec((1, D_MODEL)), _const_spec(w_up.shape), _const_spec(w_down.shape)]
    if final:
        args.append(final_norm.reshape(1, -1))
        specs.append(_const_spec((1, D_MODEL)))
    return pl.pallas_call(
        functools.partial(_post_kernel, gla=gla, final=final),
        grid=(s // tm,),
        in_specs=specs,
        out_specs=row(D_MODEL),
        out_shape=jax.ShapeDtypeStruct((s, D_MODEL), F32),
        scratch_shapes=scratch,
        compiler_params=_params("arbitrary" if gla else "parallel"),
        name="post_gla" if gla else "post_mla",
    )(*args)


GLA_LEVELS = tuple(GLA_BLOCK >> (i + 1) for i in range(int(math.log2(GLA_BLOCK))))
SUBLANES = 8
GLA_MATMUL_LEVELS = tuple(w for w in GLA_LEVELS if 2 * w < SUBLANES)


def _gla_tables():
    n = GLA_BLOCK
    t = np.arange(n)
    u = np.arange(n)[None, :]
    mats = [(u <= t[:, None])]
    for w in GLA_MATMUL_LEVELS:
        m = (t // (2 * w)) * (2 * w) + w
        upper = (t % (2 * w)) >= w
        g = np.where(upper[:, None], (u > m[:, None]) & (u <= t[:, None]),
                     (u > t[:, None]) & (u <= m[:, None]))
        mats.append(g)
    gmat = np.stack(mats).astype(np.float32)
    x = t[:, None] ^ t[None, :]
    lvl = np.where(t[:, None] > t[None, :], np.floor(np.log2(np.maximum(x, 1))), np.where(x == 0, -1, -2))
    return jnp.asarray(gmat, BF16), jnp.asarray(lvl, jnp.int32)


def _gla_proj_kernel(x_ref, g_ref, wmain_ref, wa_ref, wgk_ref, bgk_ref, gmat_ref, lvl_ref,
                     qi_out, kt_out, v_out, gate_out, dec_out, oin_out):
    n = GLA_BLOCK
    dk = GLA_DK
    xn = _rms(x_ref[...], g_ref[...]).astype(BF16)
    hm = _dot(xn, wmain_ref[...])
    a = _dot(xn, wa_ref[...])
    z = _dot(a.astype(BF16), wgk_ref[...]) + bgk_ref[...]
    gk = (jnp.minimum(z, 0.0) - jnp.log(1.0 + jnp.exp(-jnp.abs(z)))) * (1.0 / GATE_NORMALIZER)
    hi = gk.astype(BF16)
    lo = (gk - hi.astype(F32)).astype(BF16)
    hilo = jnp.concatenate([hi, lo], axis=1)

    def decay_sum(idx):
        e = _dot(gmat_ref[idx], hilo)
        return e[:, :dk] + e[:, dk:]

    b = decay_sum(0)
    b_last = b[n - 1:n, :]
    dec_out[0] = jnp.exp(b_last)
    q = hm[:, :dk] * (GLA_HEAD_K ** -0.5)
    k = hm[:, dk:2 * dk]
    vb = hm[:, 2 * dk:2 * dk + GLA_DV].astype(BF16)
    qi_out[...] = (q * jnp.exp(b)).astype(BF16)
    kt_out[...] = (k * jnp.exp(b_last - b)).astype(BF16)
    gate_out[...] = hm[:, 2 * dk + GLA_DV:]
    v_out[...] = vb

    lvl = lvl_ref[...]
    row = lax.broadcasted_iota(jnp.int32, (n, GLA_HEAD_K), 0)
    qb = q.astype(BF16)
    kb = k.astype(BF16)
    att = []
    for hd in range(GLA_HEADS):
        c = slice(hd * GLA_HEAD_K, (hd + 1) * GLA_HEAD_K)
        att.append(jnp.where(lvl == -1, _dot_nt(qb[:, c], kb[:, c]), 0.0))
    for w in GLA_LEVELS:
        seg = 2 * w
        if w in GLA_MATMUL_LEVELS:
            e = jnp.exp(decay_sum(1 + GLA_MATMUL_LEVELS.index(w)))
        else:
            b3 = b.reshape(n // seg, seg, dk)
            mid = jnp.broadcast_to(b3[:, w:w + 1, :], b3.shape).reshape(n, dk)
            e = jnp.exp(-jnp.abs(b - mid))
        upper = (row & (seg - 1)) >= w
        for hd in range(GLA_HEADS):
            c = slice(hd * GLA_HEAD_K, (hd + 1) * GLA_HEAD_K)
            xs = (jnp.where(upper, q[:, c], k[:, c]) * e[:, c]).astype(BF16)
            att[hd] = jnp.where(lvl == int(math.log2(w)), _dot_nt(xs, xs), att[hd])
    for hd in range(GLA_HEADS):
        cv = slice(hd * GLA_HEAD_V, (hd + 1) * GLA_HEAD_V)
        oin_out[:, cv] = _dot(att[hd].astype(BF16), vb[:, cv])


def _gla_proj(x, gain, w_in, w_gk_up, b_gk):
    s = x.shape[0]
    n = GLA_BLOCK
    nb = s // n
    main_w = 2 * GLA_DK + 2 * GLA_DV
    wmain = w_in[:, :main_w].astype(BF16)
    wa = jnp.pad(w_in[:, main_w:], ((0, 0), (0, LANES - GATE_RANK))).astype(BF16)
    wgk = jnp.pad(w_gk_up, ((0, LANES - GATE_RANK), (0, 0))).astype(BF16)
    gmat, lvl = _gla_tables()
    row = lambda w: pl.BlockSpec((n, w), lambda i: (i, 0))
    return pl.pallas_call(
        _gla_proj_kernel,
        grid=(nb,),
        in_specs=[row(D_MODEL), _const_spec((1, D_MODEL)), _const_spec(wmain.shape), _const_spec(wa.shape),
                  _const_spec(wgk.shape), _const_spec((1, GLA_DK)), _const_spec(gmat.shape), _const_spec(lvl.shape)],
        out_specs=[row(GLA_DK), row(GLA_DK), row(GLA_DV), row(GLA_DV),
                   pl.BlockSpec((1, 1, GLA_DK), lambda i: (i, 0, 0)), row(GLA_DV)],
        out_shape=[jax.ShapeDtypeStruct((s, GLA_DK), BF16), jax.ShapeDtypeStruct((s, GLA_DK), BF16),
                   jax.ShapeDtypeStruct((s, GLA_DV), BF16), jax.ShapeDtypeStruct((s, GLA_DV), F32),
                   jax.ShapeDtypeStruct((nb, 1, GLA_DK), F32), jax.ShapeDtypeStruct((s, GLA_DV), F32)],
        compiler_params=_params("parallel"),
        name="gla_proj",
    )(x, gain.reshape(1, -1), wmain, wa, wgk, b_gk.reshape(1, -1), gmat, lvl)


def kernel(x, positions, norm_mix, norm_mlp, mla_w_in, mla_q_norm, mla_w_uq, mla_kv_norm, mla_w_ukv, mla_w_o,
           gla_w_in, gla_w_gk_up, gla_b_gk, gla_g_norm, gla_w_o, mlp_w_up, mlp_w_down, final_norm):
    b, s, d = x.shape
    outs = []
    for bi in range(b):
        xb = x[bi]
        cos_t, sin_t = _rope_tables(positions[bi])
        for i in range(DEPTH):
            j = i // 2
            fin = final_norm if i == DEPTH - 1 else None
            if i % 2 == 0:
                q, k, vt = _mla_proj(xb, norm_mix[i], mla_w_in[j], mla_q_norm[j], mla_w_uq[j], mla_kv_norm[j],
                                     mla_w_ukv[j], cos_t, sin_t)
                o = _flash(q, k, vt)
                xb = _post(xb, o, mla_w_o[j], norm_mlp[i], mlp_w_up[i], mlp_w_down[i], final_norm=fin)
            else:
                qi, kt, v, gate, dec, oin = _gla_proj(xb, norm_mix[i], gla_w_in[j], gla_w_gk_up[j], gla_b_gk[j])
                xb = _post(xb, oin, gla_w_o[j], norm_mlp[i], mlp_w_up[i], mlp_w_down[i],
                           gla_parts=(gate, qi, kt, v, dec), g_norm=gla_g_norm[j], final_norm=fin)
        outs.append(xb)
    return jnp.stack(outs)
```

```python
import functools
import math

import jax
import jax.numpy as jnp
import numpy as np
from jax import lax
from jax.experimental import pallas as pl
from jax.experimental.pallas import tpu as pltpu

F32 = jnp.float32
BF16 = jnp.bfloat16

D_MODEL = 1024
DEPTH = 4
MLA_HEADS = 8
QK_NOPE = 128
QK_ROPE = 64
V_HEAD = 128
Q_LORA = 384
KV_LORA = 256
ROPE_THETA = 10000.0
GLA_HEADS = 4
GLA_DK = 512
GLA_DV = 1024
GLA_HEAD_K = 128
GLA_HEAD_V = 256
GATE_RANK = 16
GATE_NORMALIZER = 16.0
D_FF = 4096
EPS = 1e-6

LANES = 128
QK_PAD = 256
VMEM_LIMIT = 56 * 1024 * 1024

TM_PROJ = 256
TQ = 2048
QG = 256
FLASH_UNROLL = 2
FLASH_SKEW = 2
TM_POST = 512
GLA_BLOCK = 256
FF_CHUNK = 1024

NT_DIMS = (((1,), (1,)), ((), ()))
TN_DIMS = (((0,), (0,)), ((), ()))


def _dot(a, b):
    return jnp.dot(a, b, preferred_element_type=F32)


def _dot_nt(a, b):
    return lax.dot_general(a, b, NT_DIMS, preferred_element_type=F32)


def _dot_tn(a, b):
    return lax.dot_general(a, b, TN_DIMS, preferred_element_type=F32)


def _rms(x, gain):
    ms = jnp.mean(x * x, axis=-1, keepdims=True)
    return x * lax.rsqrt(ms + EPS) * gain


def _params(*sem):
    return pltpu.CompilerParams(dimension_semantics=sem, vmem_limit_bytes=VMEM_LIMIT)


def _const_spec(shape):
    nd = len(shape)
    return pl.BlockSpec(shape, lambda *_: (0,) * nd, pipeline_mode=pl.Buffered(1))


def _rope_kernel(pos_ref, invf_ref, sgn_ref, cos_ref, sin_ref):
    ang = pos_ref[...].astype(F32) * invf_ref[...]
    cos_ref[...] = jnp.cos(ang)
    sin_ref[...] = jnp.sin(ang) * sgn_ref[...]


def _rope_tables(positions):
    s = positions.shape[0]
    lane = np.arange(LANES)
    inv_freq = 1.0 / (ROPE_THETA ** (jnp.arange(0, QK_ROPE, 2, dtype=F32) / QK_ROPE))
    invf = inv_freq[lane % (QK_ROPE // 2)].reshape(1, LANES)
    sgn = jnp.asarray(np.where(lane % QK_ROPE < QK_ROPE // 2, -1.0, 1.0), F32).reshape(1, LANES)
    tm = 1024
    return pl.pallas_call(
        _rope_kernel,
        grid=(s // tm,),
        in_specs=[pl.BlockSpec((tm, 1), lambda i: (i, 0)),
                  pl.BlockSpec((1, LANES), lambda i: (0, 0)),
                  pl.BlockSpec((1, LANES), lambda i: (0, 0))],
        out_specs=[pl.BlockSpec((tm, LANES), lambda i: (i, 0))] * 2,
        out_shape=[jax.ShapeDtypeStruct((s, LANES), F32)] * 2,
        compiler_params=_params("parallel"),
        name="rope_tables",
    )(positions.reshape(s, 1), invf, sgn)


def _mla_proj_kernel(x_ref, g_ref, win_ref, qn_ref, kvn_ref, wq_ref, wk_ref, wvt_ref, cos_ref, sin_ref,
                     q_out, k_out, vt_out, *, q_scale):
    xn = _rms(x_ref[...], g_ref[...]).astype(BF16)
    h = _dot(xn, win_ref[...])
    cq = _rms(h[:, :Q_LORA], qn_ref[...]).astype(BF16)
    ckv = _rms(h[:, Q_LORA:Q_LORA + KV_LORA], kvn_ref[...]).astype(BF16)
    cosv = cos_ref[...]
    sinv = sin_ref[...]
    r0 = Q_LORA + KV_LORA
    k_rot = (h[:, r0:r0 + LANES] * cosv + h[:, r0 + LANES:r0 + 2 * LANES] * sinv).astype(BF16)
    qb = _dot(cq, wq_ref[...])
    kn = _dot(ckv, wk_ref[...])
    hw = MLA_HEADS * LANES
    for hd in range(MLA_HEADS):
        c = slice(hd * LANES, (hd + 1) * LANES)
        q_out[hd, :, 0:LANES] = (qb[:, c] * q_scale).astype(BF16)
        q_rot = qb[:, hw + hd * LANES:hw + (hd + 1) * LANES] * cosv \
            + qb[:, 2 * hw + hd * LANES:2 * hw + (hd + 1) * LANES] * sinv
        q_out[hd, :, LANES:QK_PAD] = (q_rot * q_scale).astype(BF16)
        k_out[hd, :, 0:LANES] = kn[:, c].astype(BF16)
        k_out[hd, :, LANES:QK_PAD] = k_rot
    vt_out[0] = _dot_nt(wvt_ref[...], ckv).astype(BF16)


def _pad_rope_cols(w, n_heads):
    k = w.shape[0]
    w = w.reshape(k, n_heads, QK_ROPE)
    half = QK_ROPE // 2
    sw = jnp.concatenate([w[..., half:], w[..., :half]], axis=-1)
    z = jnp.zeros((k, n_heads, LANES - QK_ROPE), w.dtype)
    plain = jnp.concatenate([w, z], axis=-1).reshape(k, n_heads * LANES)
    swapped = jnp.concatenate([sw, z], axis=-1).reshape(k, n_heads * LANES)
    return plain, swapped


def _mla_proj(x, gain, w_in, q_norm, w_uq, kv_norm, w_ukv, cos_t, sin_t):
    s = x.shape[0]
    tm = TM_PROJ
    r0 = Q_LORA + KV_LORA
    kr, kr_sw = _pad_rope_cols(w_in[:, r0:], 1)
    win = jnp.concatenate([w_in[:, :r0], kr, kr_sw], axis=1).astype(BF16)
    wq3 = w_uq.reshape(Q_LORA, MLA_HEADS, QK_NOPE + QK_ROPE)
    qr, qr_sw = _pad_rope_cols(wq3[..., QK_NOPE:].reshape(Q_LORA, MLA_HEADS * QK_ROPE), MLA_HEADS)
    wq = jnp.concatenate([wq3[..., :QK_NOPE].reshape(Q_LORA, -1), qr, qr_sw], axis=1).astype(BF16)
    wkv3 = w_ukv.reshape(KV_LORA, MLA_HEADS, QK_NOPE + V_HEAD)
    wk = wkv3[..., :QK_NOPE].reshape(KV_LORA, -1).astype(BF16)
    wvt = wkv3[..., QK_NOPE:].reshape(KV_LORA, -1).T.astype(BF16)
    q_scale = (QK_NOPE + QK_ROPE) ** -0.5 * math.log2(math.e)
    nt = s // tm
    return pl.pallas_call(
        functools.partial(_mla_proj_kernel, q_scale=q_scale),
        grid=(nt,),
        in_specs=[pl.BlockSpec((tm, D_MODEL), lambda i: (i, 0)),
                  _const_spec((1, D_MODEL)),
                  _const_spec(win.shape),
                  _const_spec((1, Q_LORA)),
                  _const_spec((1, KV_LORA)),
                  _const_spec(wq.shape),
                  _const_spec(wk.shape),
                  _const_spec(wvt.shape),
                  pl.BlockSpec((tm, LANES), lambda i: (i, 0)),
                  pl.BlockSpec((tm, LANES), lambda i: (i, 0))],
        out_specs=[pl.BlockSpec((MLA_HEADS, tm, QK_PAD), lambda i: (0, i, 0)),
                   pl.BlockSpec((MLA_HEADS, tm, QK_PAD), lambda i: (0, i, 0)),
                   pl.BlockSpec((1, MLA_HEADS * V_HEAD, tm), lambda i: (i, 0, 0))],
        out_shape=[jax.ShapeDtypeStruct((MLA_HEADS, s, QK_PAD), BF16),
                   jax.ShapeDtypeStruct((MLA_HEADS, s, QK_PAD), BF16),
                   jax.ShapeDtypeStruct((nt, MLA_HEADS * V_HEAD, tm), BF16)],
        compiler_params=_params("parallel"),
        name="mla_proj",
    )(x, gain.reshape(1, -1), win, q_norm.reshape(1, -1), kv_norm.reshape(1, -1), wq, wk, wvt, cos_t, sin_t)


def _flash_kernel(q_ref, k_ref, vt_ref, o_ref, m_sc, l_sc, acc_sc, s_sc, cm_sc):
    qi = pl.program_id(1)
    tk = TM_PROJ
    n_sub = TQ // tk
    n_grp = TQ // QG
    m_sc[...] = jnp.full(m_sc.shape, -jnp.inf, F32)
    l_sc[...] = jnp.zeros(l_sc.shape, F32)
    acc_sc[...] = jnp.zeros(acc_sc.shape, F32)

    def unit(g, diag):
        cols = slice(g * QG, (g + 1) * QG)
        trimmed = diag and g < n_grp - FLASH_SKEW
        rows = ((g * QG) // tk + 1) * tk if trimmed else TQ
        return cols, rows

    def scores(kstep, g, diag):
        cols, rows = unit(g, diag)
        start = pl.multiple_of(kstep * TQ, TQ)
        st = _dot_nt(k_ref[0, pl.ds(start, rows), :], q_ref[0, cols, :])
        if diag:
            kidx = lax.broadcasted_iota(jnp.int32, st.shape, 0)
            qidx = cols.start + lax.broadcasted_iota(jnp.int32, st.shape, 1)
            st = jnp.where(kidx <= qidx, st, -jnp.inf)
        s_sc[g, 0:rows, :] = st
        cm_sc[g] = jnp.max(st, axis=0, keepdims=True)

    def consume(kstep, g, diag):
        cols, rows = unit(g, diag)
        m_old = m_sc[g]
        m_new = jnp.maximum(m_old, cm_sc[g])
        alpha = jnp.exp2(m_old - m_new)
        p = jnp.exp2(s_sc[g, 0:rows, :] - m_new)
        l_sc[g] = alpha * l_sc[g] + jnp.sum(p, axis=0, keepdims=True)
        pb = p.astype(BF16)
        pv = _dot(vt_ref[kstep * n_sub], pb[0:tk])
        for jj in range(1, rows // tk):
            pv = pv + _dot(vt_ref[kstep * n_sub + jj], pb[jj * tk:(jj + 1) * tk])
        acc_sc[g] = alpha * acc_sc[g] + pv
        m_sc[g] = m_new

    for g in range(n_grp):
        scores(qi, g, True)
        if g >= FLASH_SKEW:
            consume(qi, g - FLASH_SKEW, True)

    def full_steps(first, count):
        for n in [first + d for d in range(count)]:
            prev = jnp.where(n == 0, qi, n - 1)
            for g in range(n_grp):
                scores(n, g, False)
                if g >= FLASH_SKEW:
                    consume(n, g - FLASH_SKEW, False)
                else:
                    consume(prev, g - FLASH_SKEW + n_grp, False)

    def body(t, carry):
        full_steps(t * FLASH_UNROLL, FLASH_UNROLL)
        return carry

    lax.fori_loop(0, qi // FLASH_UNROLL, body, 0)
    done = (qi // FLASH_UNROLL) * FLASH_UNROLL
    part = FLASH_UNROLL // 2
    while part >= 1:
        pl.when((qi & part) != 0)(functools.partial(full_steps, done, part))
        done = done + (qi & part)
        part //= 2
    for g in range(n_grp - FLASH_SKEW, n_grp):
        consume(jnp.where(qi == 0, qi, qi - 1), g, False)
    for g in range(n_grp):
        o = acc_sc[g] * (1.0 / l_sc[g])
        o_ref[g * QG:(g + 1) * QG, :] = o.T.astype(BF16)


def _flash(q, k, vt):
    s = q.shape[1]
    tk = TM_PROJ
    nkb = s // tk
    n_grp = TQ // QG
    return pl.pallas_call(
        _flash_kernel,
        grid=(MLA_HEADS, s // TQ),
        in_specs=[pl.BlockSpec((1, TQ, QK_PAD), lambda h, i: (h, i, 0)),
                  pl.BlockSpec((1, s, QK_PAD), lambda h, i: (h, 0, 0)),
                  pl.BlockSpec((nkb, V_HEAD, tk), lambda h, i: (0, h, 0))],
        out_specs=pl.BlockSpec((TQ, V_HEAD), lambda h, i: (i, h)),
        out_shape=jax.ShapeDtypeStruct((s, MLA_HEADS * V_HEAD), BF16),
        scratch_shapes=[pltpu.VMEM((n_grp, 1, QG), F32), pltpu.VMEM((n_grp, 1, QG), F32),
                        pltpu.VMEM((n_grp, V_HEAD, QG), F32), pltpu.VMEM((n_grp, TQ, QG), F32),
                        pltpu.VMEM((n_grp, 1, QG), F32)],
        compiler_params=_params("parallel", "arbitrary"),
        name="mla_flash",
    )(q, k, vt)


def _post_kernel(*refs, gla, final):
    if gla:
        x_ref, oin_ref, gt_ref, gn_ref, qi_ref, kt_ref, v_ref, dec_ref = refs[:8]
        refs = refs[8:]
        st_sc, o_sc = refs[-2:]
        refs = refs[:-2]
    else:
        x_ref, o_ref = refs[:2]
        refs = refs[2:]
    wo_ref, nm_ref, wup_ref, wdn_ref = refs[:4]
    fn_ref = refs[4] if final else None
    out_ref = refs[-1]

    if gla:
        @pl.when(pl.program_id(0) == 0)
        def _():
            st_sc[...] = jnp.zeros(st_sc.shape, F32)

        for blk in range(TM_POST // GLA_BLOCK):
            r = slice(blk * GLA_BLOCK, (blk + 1) * GLA_BLOCK)
            dec = dec_ref[blk]
            for hd in range(GLA_HEADS):
                ck = slice(hd * GLA_HEAD_K, (hd + 1) * GLA_HEAD_K)
                cv = slice(hd * GLA_HEAD_V, (hd + 1) * GLA_HEAD_V)
                st = st_sc[hd]
                o_sc[r, cv] = oin_ref[r, cv] + _dot_nt(qi_ref[r, ck], st.astype(BF16))
                st_sc[hd] = st * dec[:, ck] + _dot_tn(v_ref[r, cv], kt_ref[r, ck])
        gn = gn_ref[...]
        parts = []
        for hd in range(GLA_HEADS):
            c = slice(hd * GLA_HEAD_V, (hd + 1) * GLA_HEAD_V)
            g = gt_ref[:, c]
            parts.append((_rms(o_sc[:, c], gn) * (g * (1.0 / (1.0 + jnp.exp(-g))))).astype(BF16))
        ob = jnp.concatenate(parts, axis=1)
    else:
        ob = o_ref[...]
    x1 = x_ref[...] + _dot(ob, wo_ref[...])
    xn = _rms(x1, nm_ref[...]).astype(BF16)
    y = jnp.zeros_like(x1)
    for c in range(D_FF // FF_CHUNK):
        hid = _dot(xn, wup_ref[:, c * FF_CHUNK:(c + 1) * FF_CHUNK])
        hid = jnp.square(jnp.maximum(hid, 0.0)).astype(BF16)
        y = y + _dot(hid, wdn_ref[c * FF_CHUNK:(c + 1) * FF_CHUNK, :])
    x2 = x1 + y
    if final:
        x2 = _rms(x2, fn_ref[...])
    out_ref[...] = x2


def _post(x, o, w_o, norm_mlp, w_up, w_down, *, gla_parts=None, g_norm=None, final_norm=None):
    s = x.shape[0]
    tm = TM_POST
    gla = gla_parts is not None
    final = final_norm is not None
    row = lambda w: pl.BlockSpec((tm, w), lambda i: (i, 0))
    args = [x, o]
    specs = [row(D_MODEL), row(D_MODEL)]
    scratch = []
    if gla:
        gate, qi, kt, v, dec = gla_parts
        nblk = tm // GLA_BLOCK
        args += [gate, g_norm.reshape(1, -1), qi, kt, v, dec]
        specs += [row(GLA_DV), _const_spec((1, GLA_HEAD_V)), row(GLA_DK), row(GLA_DK), row(GLA_DV),
                  pl.BlockSpec((nblk, 1, GLA_DK), lambda i: (i, 0, 0))]
        scratch = [pltpu.VMEM((GLA_HEADS, GLA_HEAD_V, GLA_HEAD_K), F32), pltpu.VMEM((tm, GLA_DV), F32)]
    args += [w_o.astype(BF16), norm_mlp.reshape(1, -1), w_up.astype(BF16), w_down.astype(BF16)]
    specs += [_const_spec(w_o.shape), _const_spec((1, D_MODEL)), _const_spec(w_up.shape), _const_spec(w_down.shape)]
    if final:
        args.append(final_norm.reshape(1, -1))
        specs.append(_const_spec((1, D_MODEL)))
    return pl.pallas_call(
        functools.partial(_post_kernel, gla=gla, final=final),
        grid=(s // tm,),
        in_specs=specs,
        out_specs=row(D_MODEL),
        out_shape=jax.ShapeDtypeStruct((s, D_MODEL), F32),
        scratch_shapes=scratch,
        compiler_params=_params("arbitrary" if gla else "parallel"),
        name="post_gla" if gla else "post_mla",
    )(*args)


GLA_LEVELS = tuple(GLA_BLOCK >> (i + 1) for i in range(int(math.log2(GLA_BLOCK))))
SUBLANES = 8
GLA_MATMUL_LEVELS = tuple(w for w in GLA_LEVELS if 2 * w < SUBLANES)


def _gla_tables():
    n = GLA_BLOCK
    t = np.arange(n)
    u = np.arange(n)[None, :]
    mats = [(u <= t[:, None])]
    for w in GLA_MATMUL_LEVELS:
        m = (t // (2 * w)) * (2 * w) + w
        upper = (t % (2 * w)) >= w
        g = np.where(upper[:, None], (u > m[:, None]) & (u <= t[:, None]),
                     (u > t[:, None]) & (u <= m[:, None]))
        mats.append(g)
    gmat = np.stack(mats).astype(np.float32)
    x = t[:, None] ^ t[None, :]
    lvl = np.where(t[:, None] > t[None, :], np.floor(np.log2(np.maximum(x, 1))), np.where(x == 0, -1, -2))
    return jnp.asarray(gmat, BF16), jnp.asarray(lvl, jnp.int32)


def _gla_proj_kernel(x_ref, g_ref, wmain_ref, wa_ref, wgk_ref, bgk_ref, gmat_ref, lvl_ref,
                     qi_out, kt_out, v_out, gate_out, dec_out, oin_out):
    n = GLA_BLOCK
    dk = GLA_DK
    xn = _rms(x_ref[...], g_ref[...]).astype(BF16)
    hm = _dot(xn, wmain_ref[...])
    a = _dot(xn, wa_ref[...])
    z = _dot(a.astype(BF16), wgk_ref[...]) + bgk_ref[...]
    gk = (jnp.minimum(z, 0.0) - jnp.log(1.0 + jnp.exp(-jnp.abs(z)))) * (1.0 / GATE_NORMALIZER)
    hi = gk.astype(BF16)
    lo = (gk - hi.astype(F32)).astype(BF16)
    hilo = jnp.concatenate([hi, lo], axis=1)

    def decay_sum(idx):
        e = _dot(gmat_ref[idx], hilo)
        return e[:, :dk] + e[:, dk:]

    b = decay_sum(0)
    b_last = b[n - 1:n, :]
    dec_out[0] = jnp.exp(b_last)
    q = hm[:, :dk] * (GLA_HEAD_K ** -0.5)
    k = hm[:, dk:2 * dk]
    vb = hm[:, 2 * dk:2 * dk + GLA_DV].astype(BF16)
    qi_out[...] = (q * jnp.exp(b)).astype(BF16)
    kt_out[...] = (k * jnp.exp(b_last - b)).astype(BF16)
    gate_out[...] = hm[:, 2 * dk + GLA_DV:]
    v_out[...] = vb

    lvl = lvl_ref[...]
    row = lax.broadcasted_iota(jnp.int32, (n, GLA_HEAD_K), 0)
    qb = q.astype(BF16)
    kb = k.astype(BF16)
    att = []
    for hd in range(GLA_HEADS):
        c = slice(hd * GLA_HEAD_K, (hd + 1) * GLA_HEAD_K)
        att.append(jnp.where(lvl == -1, _dot_nt(qb[:, c], kb[:, c]), 0.0))
    for w in GLA_LEVELS:
        seg = 2 * w
        if w in GLA_MATMUL_LEVELS:
            e = jnp.exp(decay_sum(1 + GLA_MATMUL_LEVELS.index(w)))
        else:
            b3 = b.reshape(n // seg, seg, dk)
            mid = jnp.broadcast_to(b3[:, w:w + 1, :], b3.shape).reshape(n, dk)
            e = jnp.exp(-jnp.abs(b - mid))
        upper = (row & (seg - 1)) >= w
        for hd in range(GLA_HEADS):
            c = slice(hd * GLA_HEAD_K, (hd + 1) * GLA_HEAD_K)
            xs = (jnp.where(upper, q[:, c], k[:, c]) * e[:, c]).astype(BF16)
            att[hd] = jnp.where(lvl == int(math.log2(w)), _dot_nt(xs, xs), att[hd])
    for hd in range(GLA_HEADS):
        cv = slice(hd * GLA_HEAD_V, (hd + 1) * GLA_HEAD_V)
        oin_out[:, cv] = _dot(att[hd].astype(BF16), vb[:, cv])


def _gla_proj(x, gain, w_in, w_gk_up, b_gk):
    s = x.shape[0]
    n = GLA_BLOCK
    nb = s // n
    main_w = 2 * GLA_DK + 2 * GLA_DV
    wmain = w_in[:, :main_w].astype(BF16)
    wa = jnp.pad(w_in[:, main_w:], ((0, 0), (0, LANES - GATE_RANK))).astype(BF16)
    wgk = jnp.pad(w_gk_up, ((0, LANES - GATE_RANK), (0, 0))).astype(BF16)
    gmat, lvl = _gla_tables()
    row = lambda w: pl.BlockSpec((n, w), lambda i: (i, 0))
    return pl.pallas_call(
        _gla_proj_kernel,
        grid=(nb,),
        in_specs=[row(D_MODEL), _const_spec((1, D_MODEL)), _const_spec(wmain.shape), _const_spec(wa.shape),
                  _const_spec(wgk.shape), _const_spec((1, GLA_DK)), _const_spec(gmat.shape), _const_spec(lvl.shape)],
        out_specs=[row(GLA_DK), row(GLA_DK), row(GLA_DV), row(GLA_DV),
                   pl.BlockSpec((1, 1, GLA_DK), lambda i: (i, 0, 0)), row(GLA_DV)],
        out_shape=[jax.ShapeDtypeStruct((s, GLA_DK), BF16), jax.ShapeDtypeStruct((s, GLA_DK), BF16),
                   jax.ShapeDtypeStruct((s, GLA_DV), BF16), jax.ShapeDtypeStruct((s, GLA_DV), F32),
                   jax.ShapeDtypeStruct((nb, 1, GLA_DK), F32), jax.ShapeDtypeStruct((s, GLA_DV), F32)],
        compiler_params=_params("parallel"),
        name="gla_proj",
    )(x, gain.reshape(1, -1), wmain, wa, wgk, b_gk.reshape(1, -1), gmat, lvl)


def kernel(x, positions, norm_mix, norm_mlp, mla_w_in, mla_q_norm, mla_w_uq, mla_kv_norm, mla_w_ukv, mla_w_o,
           gla_w_in, gla_w_gk_up, gla_b_gk, gla_g_norm, gla_w_o, mlp_w_up, mlp_w_down, final_norm):
    b, s, d = x.shape
    outs = []
    for bi in range(b):
        xb = x[bi]
        cos_t, sin_t = _rope_tables(positions[bi])
        for i in range(DEPTH):
            j = i // 2
            fin = final_norm if i == DEPTH - 1 else None
            if i % 2 == 0:
                q, k, vt = _mla_proj(xb, norm_mix[i], mla_w_in[j], mla_q_norm[j], mla_w_uq[j], mla_kv_norm[j],
                                     mla_w_ukv[j], cos_t, sin_t)
                o = _flash(q, k, vt)
                xb = _post(xb, o, mla_w_o[j], norm_mlp[i], mlp_w_up[i], mlp_w_down[i], final_norm=fin)
            else:
                qi, kt, v, gate, dec, oin = _gla_proj(xb, norm_mix[i], gla_w_in[j], gla_w_gk_up[j], gla_b_gk[j])
                xb = _post(xb, oin, gla_w_o[j], norm_mlp[i], mlp_w_up[i], mlp_w_down[i],
                           gla_parts=(gate, qi, kt, v, dec), g_norm=gla_g_norm[j], final_norm=fin)
        outs.append(xb)
    return jnp.stack(outs)
```

```python
import functools
import math

import jax
import jax.numpy as jnp
import numpy as np
from jax import lax
from jax.experimental import pallas as pl
from jax.experimental.pallas import tpu as pltpu

F32 = jnp.float32
BF16 = jnp.bfloat16

D_MODEL = 1024
DEPTH = 4
MLA_HEADS = 8
QK_NOPE = 128
QK_ROPE = 64
V_HEAD = 128
Q_LORA = 384
KV_LORA = 256
ROPE_THETA = 10000.0
GLA_HEADS = 4
GLA_DK = 512
GLA_DV = 1024
GLA_HEAD_K = 128
GLA_HEAD_V = 256
GATE_RANK = 16
GATE_NORMALIZER = 16.0
D_FF = 4096
EPS = 1e-6

LANES = 128
QK_PAD = 256
VMEM_LIMIT = 56 * 1024 * 1024

TM_PROJ = 256
TQ = 2048
QG = 256
FLASH_UNROLL = 2
FLASH_SKEW = 2
TM_POST = 512
GLA_BLOCK = 256
FF_CHUNK = 1024

NT_DIMS = (((1,), (1,)), ((), ()))
TN_DIMS = (((0,), (0,)), ((), ()))


def _dot(a, b):
    return jnp.dot(a, b, preferred_element_type=F32)


def _dot_nt(a, b):
    return lax.dot_general(a, b, NT_DIMS, preferred_element_type=F32)


def _dot_tn(a, b):
    return lax.dot_general(a, b, TN_DIMS, preferred_element_type=F32)


def _rms(x, gain):
    ms = jnp.mean(x * x, axis=-1, keepdims=True)
    return x * lax.rsqrt(ms + EPS) * gain


def _params(*sem):
    return pltpu.CompilerParams(dimension_semantics=sem, vmem_limit_bytes=VMEM_LIMIT)


def _const_spec(shape):
    nd = len(shape)
    return pl.BlockSpec(shape, lambda *_: (0,) * nd, pipeline_mode=pl.Buffered(1))


def _rope_kernel(pos_ref, invf_ref, sgn_ref, cos_ref, sin_ref):
    ang = pos_ref[...].astype(F32) * invf_ref[...]
    cos_ref[...] = jnp.cos(ang)
    sin_ref[...] = jnp.sin(ang) * sgn_ref[...]


def _rope_tables(positions):
    s = positions.shape[0]
    lane = np.arange(LANES)
    inv_freq = 1.0 / (ROPE_THETA ** (jnp.arange(0, QK_ROPE, 2, dtype=F32) / QK_ROPE))
    invf = inv_freq[lane % (QK_ROPE // 2)].reshape(1, LANES)
    sgn = jnp.asarray(np.where(lane % QK_ROPE < QK_ROPE // 2, -1.0, 1.0), F32).reshape(1, LANES)
    tm = 1024
    return pl.pallas_call(
        _rope_kernel,
        grid=(s // tm,),
        in_specs=[pl.BlockSpec((tm, 1), lambda i: (i, 0)),
                  pl.BlockSpec((1, LANES), lambda i: (0, 0)),
                  pl.BlockSpec((1, LANES), lambda i: (0, 0))],
        out_specs=[pl.BlockSpec((tm, LANES), lambda i: (i, 0))] * 2,
        out_shape=[jax.ShapeDtypeStruct((s, LANES), F32)] * 2,
        compiler_params=_params("parallel"),
        name="rope_tables",
    )(positions.reshape(s, 1), invf, sgn)


def _mla_proj_kernel(x_ref, g_ref, win_ref, qn_ref, kvn_ref, wq_ref, wk_ref, wvt_ref, cos_ref, sin_ref,
                     q_out, k_out, vt_out, *, q_scale):
    xn = _rms(x_ref[...], g_ref[...]).astype(BF16)
    h = _dot(xn, win_ref[...])
    cq = _rms(h[:, :Q_LORA], qn_ref[...]).astype(BF16)
    ckv = _rms(h[:, Q_LORA:Q_LORA + KV_LORA], kvn_ref[...]).astype(BF16)
    cosv = cos_ref[...]
    sinv = sin_ref[...]
    r0 = Q_LORA + KV_LORA
    k_rot = (h[:, r0:r0 + LANES] * cosv + h[:, r0 + LANES:r0 + 2 * LANES] * sinv).astype(BF16)
    qb = _dot(cq, wq_ref[...])
    kn = _dot(ckv, wk_ref[...])
    hw = MLA_HEADS * LANES
    for hd in range(MLA_HEADS):
        c = slice(hd * LANES, (hd + 1) * LANES)
        q_out[hd, :, 0:LANES] = (qb[:, c] * q_scale).astype(BF16)
        q_rot = qb[:, hw + hd * LANES:hw + (hd + 1) * LANES] * cosv \
            + qb[:, 2 * hw + hd * LANES:2 * hw + (hd + 1) * LANES] * sinv
        q_out[hd, :, LANES:QK_PAD] = (q_rot * q_scale).astype(BF16)
        k_out[hd, :, 0:LANES] = kn[:, c].astype(BF16)
        k_out[hd, :, LANES:QK_PAD] = k_rot
    vt_out[0] = _dot_nt(wvt_ref[...], ckv).astype(BF16)


def _pad_rope_cols(w, n_heads):
    k = w.shape[0]
    w = w.reshape(k, n_heads, QK_ROPE)
    half = QK_ROPE // 2
    sw = jnp.concatenate([w[..., half:], w[..., :half]], axis=-1)
    z = jnp.zeros((k, n_heads, LANES - QK_ROPE), w.dtype)
    plain = jnp.concatenate([w, z], axis=-1).reshape(k, n_heads * LANES)
    swapped = jnp.concatenate([sw, z], axis=-1).reshape(k, n_heads * LANES)
    return plain, swapped


def _mla_proj(x, gain, w_in, q_norm, w_uq, kv_norm, w_ukv, cos_t, sin_t):
    s = x.shape[0]
    tm = TM_PROJ
    r0 = Q_LORA + KV_LORA
    kr, kr_sw = _pad_rope_cols(w_in[:, r0:], 1)
    win = jnp.concatenate([w_in[:, :r0], kr, kr_sw], axis=1).astype(BF16)
    wq3 = w_uq.reshape(Q_LORA, MLA_HEADS, QK_NOPE + QK_ROPE)
    qr, qr_sw = _pad_rope_cols(wq3[..., QK_NOPE:].reshape(Q_LORA, MLA_HEADS * QK_ROPE), MLA_HEADS)
    wq = jnp.concatenate([wq3[..., :QK_NOPE].reshape(Q_LORA, -1), qr, qr_sw], axis=1).astype(BF16)
    wkv3 = w_ukv.reshape(KV_LORA, MLA_HEADS, QK_NOPE + V_HEAD)
    wk = wkv3[..., :QK_NOPE].reshape(KV_LORA, -1).astype(BF16)
    wvt = wkv3[..., QK_NOPE:].reshape(KV_LORA, -1).T.astype(BF16)
    q_scale = (QK_NOPE + QK_ROPE) ** -0.5 * math.log2(math.e)
    nt = s // tm
    return pl.pallas_call(
        functools.partial(_mla_proj_kernel, q_scale=q_scale),
        grid=(nt,),
        in_specs=[pl.BlockSpec((tm, D_MODEL), lambda i: (i, 0)),
                  _const_spec((1, D_MODEL)),
                  _const_spec(win.shape),
                  _const_spec((1, Q_LORA)),
                  _const_spec((1, KV_LORA)),
                  _const_spec(wq.shape),
                  _const_spec(wk.shape),
                  _const_spec(wvt.shape),
                  pl.BlockSpec((tm, LANES), lambda i: (i, 0)),
                  pl.BlockSpec((tm, LANES), lambda i: (i, 0))],
        out_specs=[pl.BlockSpec((MLA_HEADS, tm, QK_PAD), lambda i: (0, i, 0)),
                   pl.BlockSpec((MLA_HEADS, tm, QK_PAD), lambda i: (0, i, 0)),
                   pl.BlockSpec((1, MLA_HEADS * V_HEAD, tm), lambda i: (i, 0, 0))],
        out_shape=[jax.ShapeDtypeStruct((MLA_HEADS, s, QK_PAD), BF16),
                   jax.ShapeDtypeStruct((MLA_HEADS, s, QK_PAD), BF16),
                   jax.ShapeDtypeStruct((nt, MLA_HEADS * V_HEAD, tm), BF16)],
        compiler_params=_params("parallel"),
        name="mla_proj",
    )(x, gain.reshape(1, -1), win, q_norm.reshape(1, -1), kv_norm.reshape(1, -1), wq, wk, wvt, cos_t, sin_t)


def _flash_kernel(q_ref, k_ref, vt_ref, o_ref, m_sc, l_sc, acc_sc, s_sc, cm_sc):
    qi = pl.program_id(1)
    tk = TM_PROJ
    n_sub = TQ // tk
    n_grp = TQ // QG
    m_sc[...] = jnp.full(m_sc.shape, -jnp.inf, F32)
    l_sc[...] = jnp.zeros(l_sc.shape, F32)
    acc_sc[...] = jnp.zeros(acc_sc.shape, F32)

    def unit(g, diag):
        cols = slice(g * QG, (g + 1) * QG)
        trimmed = diag and g < n_grp - FLASH_SKEW
        rows = ((g * QG) // tk + 1) * tk if trimmed else TQ
        return cols, rows

    def scores(kstep, g, diag):
        cols, rows = unit(g, diag)
        start = pl.multiple_of(kstep * TQ, TQ)
        st = _dot_nt(k_ref[0, pl.ds(start, rows), :], q_ref[0, cols, :])
        if diag:
            kidx = lax.broadcasted_iota(jnp.int32, st.shape, 0)
            qidx = cols.start + lax.broadcasted_iota(jnp.int32, st.shape, 1)
            st = jnp.where(kidx <= qidx, st, -jnp.inf)
        s_sc[g, 0:rows, :] = st
        cm_sc[g] = jnp.max(st, axis=0, keepdims=True)

    def consume(kstep, g, diag):
        cols, rows = unit(g, diag)
        m_old = m_sc[g]
        m_new = jnp.maximum(m_old, cm_sc[g])
        alpha = jnp.exp2(m_old - m_new)
        p = jnp.exp2(s_sc[g, 0:rows, :] - m_new)
        l_sc[g] = alpha * l_sc[g] + jnp.sum(p, axis=0, keepdims=True)
        pb = p.astype(BF16)
        pv = _dot(vt_ref[kstep * n_sub], pb[0:tk])
        for jj in range(1, rows // tk):
            pv = pv + _dot(vt_ref[kstep * n_sub + jj], pb[jj * tk:(jj + 1) * tk])
        acc_sc[g] = alpha * acc_sc[g] + pv
        m_sc[g] = m_new

    for g in range(n_grp):
        scores(qi, g, True)
        if g >= FLASH_SKEW:
            consume(qi, g - FLASH_SKEW, True)

    def full_steps(first, count):
        for n in [first + d for d in range(count)]:
            prev = jnp.where(n == 0, qi, n - 1)
            for g in range(n_grp):
                scores(n, g, False)
                if g >= FLASH_SKEW:
                    consume(n, g - FLASH_SKEW, False)
                else:
                    consume(prev, g - FLASH_SKEW + n_grp, False)

    def body(t, carry):
        full_steps(t * FLASH_UNROLL, FLASH_UNROLL)
        return carry

    lax.fori_loop(0, qi // FLASH_UNROLL, body, 0)
    done = (qi // FLASH_UNROLL) * FLASH_UNROLL
    part = FLASH_UNROLL // 2
    while part >= 1:
        pl.when((qi & part) != 0)(functools.partial(full_steps, done, part))
        done = done + (qi & part)
        part //= 2
    for g in range(n_grp - FLASH_SKEW, n_grp):
        consume(jnp.where(qi == 0, qi, qi - 1), g, False)
    for g in range(n_grp):
        o = acc_sc[g] * (1.0 / l_sc[g])
        o_ref[g * QG:(g + 1) * QG, :] = o.T.astype(BF16)


def _flash(q, k, vt):
    s = q.shape[1]
    tk = TM_PROJ
    nkb = s // tk
    n_grp = TQ // QG
    return pl.pallas_call(
        _flash_kernel,
        grid=(MLA_HEADS, s // TQ),
        in_specs=[pl.BlockSpec((1, TQ, QK_PAD), lambda h, i: (h, i, 0)),
                  pl.BlockSpec((1, s, QK_PAD), lambda h, i: (h, 0, 0)),
                  pl.BlockSpec((nkb, V_HEAD, tk), lambda h, i: (0, h, 0))],
        out_specs=pl.BlockSpec((TQ, V_HEAD), lambda h, i: (i, h)),
        out_shape=jax.ShapeDtypeStruct((s, MLA_HEADS * V_HEAD), BF16),
        scratch_shapes=[pltpu.VMEM((n_grp, 1, QG), F32), pltpu.VMEM((n_grp, 1, QG), F32),
                        pltpu.VMEM((n_grp, V_HEAD, QG), F32), pltpu.VMEM((n_grp, TQ, QG), F32),
                        pltpu.VMEM((n_grp, 1, QG), F32)],
        compiler_params=_params("parallel", "arbitrary"),
        name="mla_flash",
    )(q, k, vt)


def _post_kernel(*refs, gla, final):
    if gla:
        x_ref, oin_ref, gt_ref, gn_ref, qi_ref, kt_ref, v_ref, dec_ref = refs[:8]
        refs = refs[8:]
        st_sc, o_sc = refs[-2:]
        refs = refs[:-2]
    else:
        x_ref, o_ref = refs[:2]
        refs = refs[2:]
    wo_ref, nm_ref, wup_ref, wdn_ref = refs[:4]
    fn_ref = refs[4] if final else None
    out_ref = refs[-1]

    if gla:
        @pl.when(pl.program_id(0) == 0)
        def _():
            st_sc[...] = jnp.zeros(st_sc.shape, F32)

        for blk in range(TM_POST // GLA_BLOCK):
            r = slice(blk * GLA_BLOCK, (blk + 1) * GLA_BLOCK)
            dec = dec_ref[blk]
            for hd in range(GLA_HEADS):
                ck = slice(hd * GLA_HEAD_K, (hd + 1) * GLA_HEAD_K)
                cv = slice(hd * GLA_HEAD_V, (hd + 1) * GLA_HEAD_V)
                st = st_sc[hd]
                o_sc[r, cv] = oin_ref[r, cv] + _dot_nt(qi_ref[r, ck], st.astype(BF16))
                st_sc[hd] = st * dec[:, ck] + _dot_tn(v_ref[r, cv], kt_ref[r, ck])
        gn = gn_ref[...]
        parts = []
        for hd in range(GLA_HEADS):
            c = slice(hd * GLA_HEAD_V, (hd + 1) * GLA_HEAD_V)
            g = gt_ref[:, c]
            parts.append((_rms(o_sc[:, c], gn) * (g * (1.0 / (1.0 + jnp.exp(-g))))).astype(BF16))
        ob = jnp.concatenate(parts, axis=1)
    else:
        ob = o_ref[...]
    x1 = x_ref[...] + _dot(ob, wo_ref[0])
    xn = _rms(x1, nm_ref[...]).astype(BF16)
    y = jnp.zeros_like(x1)
    for c in range(D_FF // FF_CHUNK):
        hid = _dot(xn, wup_ref[0, :, c * FF_CHUNK:(c + 1) * FF_CHUNK])
        hid = jnp.square(jnp.maximum(hid, 0.0)).astype(BF16)
        y = y + _dot(hid, wdn_ref[0, c * FF_CHUNK:(c + 1) * FF_CHUNK, :])
    x2 = x1 + y
    if final:
        x2 = _rms(x2, fn_ref[...])
    out_ref[...] = x2


def _layer_spec(stacked_shape, layer):
    block = (1,) + tuple(stacked_shape[1:])
    return pl.BlockSpec(block, lambda *_: (layer,) + (0,) * (len(block) - 1), pipeline_mode=pl.Buffered(1))


def _post(x, o, w_o, norm_mlp, w_up, w_down, *, gla_parts=None, g_norm=None, final_norm=None):
    s = x.shape[0]
    tm = TM_POST
    gla = gla_parts is not None
    final = final_norm is not None
    row = lambda w: pl.BlockSpec((tm, w), lambda i: (i, 0))
    args = [x, o]
    specs = [row(D_MODEL), row(D_MODEL)]
    scratch = []
    if gla:
        gate, qi, kt, v, dec = gla_parts
        nblk = tm // GLA_BLOCK
        args += [gate, g_norm.reshape(1, -1), qi, kt, v, dec]
        specs += [row(GLA_DV), _const_spec((1, GLA_HEAD_V)), row(GLA_DK), row(GLA_DK), row(GLA_DV),
                  pl.BlockSpec((nblk, 1, GLA_DK), lambda i: (i, 0, 0))]
        scratch = [pltpu.VMEM((GLA_HEADS, GLA_HEAD_V, GLA_HEAD_K), F32), pltpu.VMEM((tm, GLA_DV), F32)]
    args += [w_o[0], norm_mlp.reshape(1, -1), w_up[0], w_down[0]]
    specs += [_layer_spec(w_o[0].shape, w_o[1]), _const_spec((1, D_MODEL)),
              _layer_spec(w_up[0].shape, w_up[1]), _layer_spec(w_down[0].shape, w_down[1])]
    if final:
        args.append(final_norm.reshape(1, -1))
        specs.append(_const_spec((1, D_MODEL)))
    return pl.pallas_call(
        functools.partial(_post_kernel, gla=gla, final=final),
        grid=(s // tm,),
        in_specs=specs,
        out_specs=row(D_MODEL),
        out_shape=jax.ShapeDtypeStruct((s, D_MODEL), F32),
        scratch_shapes=scratch,
        compiler_params=_params("arbitrary" if gla else "parallel"),
        name="post_gla" if gla else "post_mla",
    )(*args)


GLA_LEVELS = tuple(GLA_BLOCK >> (i + 1) for i in range(int(math.log2(GLA_BLOCK))))
SUBLANES = 8
GLA_MATMUL_LEVELS = tuple(w for w in GLA_LEVELS if 2 * w < SUBLANES)


def _gla_tables():
    n = GLA_BLOCK
    t = np.arange(n)
    u = np.arange(n)[None, :]
    mats = [(u <= t[:, None])]
    for w in GLA_MATMUL_LEVELS:
        m = (t // (2 * w)) * (2 * w) + w
        upper = (t % (2 * w)) >= w
        g = np.where(upper[:, None], (u > m[:, None]) & (u <= t[:, None]),
                     (u > t[:, None]) & (u <= m[:, None]))
        mats.append(g)
    gmat = np.stack(mats).astype(np.float32)
    x = t[:, None] ^ t[None, :]
    lvl = np.where(t[:, None] > t[None, :], np.floor(np.log2(np.maximum(x, 1))), np.where(x == 0, -1, -2))
    return jnp.asarray(gmat, BF16), jnp.asarray(lvl, jnp.int32)


def _gla_proj_kernel(x_ref, g_ref, wmain_ref, wa_ref, wgk_ref, bgk_ref, gmat_ref, lvl_ref,
                     qi_out, kt_out, v_out, gate_out, dec_out, oin_out):
    n = GLA_BLOCK
    dk = GLA_DK
    xn = _rms(x_ref[...], g_ref[...]).astype(BF16)
    hm = _dot(xn, wmain_ref[...])
    a = _dot(xn, wa_ref[...])
    z = _dot(a.astype(BF16), wgk_ref[...]) + bgk_ref[...]
    gk = (jnp.minimum(z, 0.0) - jnp.log(1.0 + jnp.exp(-jnp.abs(z)))) * (1.0 / GATE_NORMALIZER)
    hi = gk.astype(BF16)
    lo = (gk - hi.astype(F32)).astype(BF16)
    hilo = jnp.concatenate([hi, lo], axis=1)

    def decay_sum(idx):
        e = _dot(gmat_ref[idx], hilo)
        return e[:, :dk] + e[:, dk:]

    b = decay_sum(0)
    b_last = b[n - 1:n, :]
    dec_out[0] = jnp.exp(b_last)
    q = hm[:, :dk] * (GLA_HEAD_K ** -0.5)
    k = hm[:, dk:2 * dk]
    vb = hm[:, 2 * dk:2 * dk + GLA_DV].astype(BF16)
    qi_out[...] = (q * jnp.exp(b)).astype(BF16)
    kt_out[...] = (k * jnp.exp(b_last - b)).astype(BF16)
    gate_out[...] = hm[:, 2 * dk + GLA_DV:]
    v_out[...] = vb

    lvl = lvl_ref[...]
    row = lax.broadcasted_iota(jnp.int32, (n, GLA_HEAD_K), 0)
    qb = q.astype(BF16)
    kb = k.astype(BF16)
    att = []
    for hd in range(GLA_HEADS):
        c = slice(hd * GLA_HEAD_K, (hd + 1) * GLA_HEAD_K)
        att.append(jnp.where(lvl == -1, _dot_nt(qb[:, c], kb[:, c]), 0.0))
    for w in GLA_LEVELS:
        seg = 2 * w
        if w in GLA_MATMUL_LEVELS:
            e = jnp.exp(decay_sum(1 + GLA_MATMUL_LEVELS.index(w)))
        else:
            b3 = b.reshape(n // seg, seg, dk)
            mid = jnp.broadcast_to(b3[:, w:w + 1, :], b3.shape).reshape(n, dk)
            e = jnp.exp(-jnp.abs(b - mid))
        upper = (row & (seg - 1)) >= w
        for hd in range(GLA_HEADS):
            c = slice(hd * GLA_HEAD_K, (hd + 1) * GLA_HEAD_K)
            xs = (jnp.where(upper, q[:, c], k[:, c]) * e[:, c]).astype(BF16)
            att[hd] = jnp.where(lvl == int(math.log2(w)), _dot_nt(xs, xs), att[hd])
    for hd in range(GLA_HEADS):
        cv = slice(hd * GLA_HEAD_V, (hd + 1) * GLA_HEAD_V)
        oin_out[:, cv] = _dot(att[hd].astype(BF16), vb[:, cv])


def _gla_proj(x, gain, w_in, w_gk_up, b_gk):
    s = x.shape[0]
    n = GLA_BLOCK
    nb = s // n
    main_w = 2 * GLA_DK + 2 * GLA_DV
    wmain = w_in[:, :main_w].astype(BF16)
    wa = jnp.pad(w_in[:, main_w:], ((0, 0), (0, LANES - GATE_RANK))).astype(BF16)
    wgk = jnp.pad(w_gk_up, ((0, LANES - GATE_RANK), (0, 0))).astype(BF16)
    gmat, lvl = _gla_tables()
    row = lambda w: pl.BlockSpec((n, w), lambda i: (i, 0))
    return pl.pallas_call(
        _gla_proj_kernel,
        grid=(nb,),
        in_specs=[row(D_MODEL), _const_spec((1, D_MODEL)), _const_spec(wmain.shape), _const_spec(wa.shape),
                  _const_spec(wgk.shape), _const_spec((1, GLA_DK)), _const_spec(gmat.shape), _const_spec(lvl.shape)],
        out_specs=[row(GLA_DK), row(GLA_DK), row(GLA_DV), row(GLA_DV),
                   pl.BlockSpec((1, 1, GLA_DK), lambda i: (i, 0, 0)), row(GLA_DV)],
        out_shape=[jax.ShapeDtypeStruct((s, GLA_DK), BF16), jax.ShapeDtypeStruct((s, GLA_DK), BF16),
                   jax.ShapeDtypeStruct((s, GLA_DV), BF16), jax.ShapeDtypeStruct((s, GLA_DV), F32),
                   jax.ShapeDtypeStruct((nb, 1, GLA_DK), F32), jax.ShapeDtypeStruct((s, GLA_DV), F32)],
        compiler_params=_params("parallel"),
        name="gla_proj",
    )(x, gain.reshape(1, -1), wmain, wa, wgk, b_gk.reshape(1, -1), gmat, lvl)


def kernel(x, positions, norm_mix, norm_mlp, mla_w_in, mla_q_norm, mla_w_uq, mla_kv_norm, mla_w_ukv, mla_w_o,
           gla_w_in, gla_w_gk_up, gla_b_gk, gla_g_norm, gla_w_o, mlp_w_up, mlp_w_down, final_norm):
    b, s, d = x.shape
    w_up_b, w_down_b = mlp_w_up.astype(BF16), mlp_w_down.astype(BF16)
    mla_wo_b, gla_wo_b = mla_w_o.astype(BF16), gla_w_o.astype(BF16)
    outs = []
    for bi in range(b):
        xb = x[bi]
        cos_t, sin_t = _rope_tables(positions[bi])
        for i in range(DEPTH):
            j = i // 2
            fin = final_norm if i == DEPTH - 1 else None
            if i % 2 == 0:
                q, k, vt = _mla_proj(xb, norm_mix[i], mla_w_in[j], mla_q_norm[j], mla_w_uq[j], mla_kv_norm[j],
                                     mla_w_ukv[j], cos_t, sin_t)
                o = _flash(q, k, vt)
                xb = _post(xb, o, (mla_wo_b, j), norm_mlp[i], (w_up_b, i), (w_down_b, i), final_norm=fin)
            else:
                qi, kt, v, gate, dec, oin = _gla_proj(xb, norm_mix[i], gla_w_in[j], gla_w_gk_up[j], gla_b_gk[j])
                xb = _post(xb, oin, (gla_wo_b, j), norm_mlp[i], (w_up_b, i), (w_down_b, i),
                           gla_parts=(gate, qi, kt, v, dec), g_norm=gla_g_norm[j], final_norm=fin)
        outs.append(xb)
    return jnp.stack(outs)
```
